```python
import jax, jax.numpy as jnp
from jax import lax
import numpy as np

D_MODEL = 1024
BATCH = 8
SEQ = 2048
DEPTH = 2
DEC_BATCH = 128
DEC_SEQ = 8
PAST_LEN = 16384
PAGE_SIZE = 128

N_RET_HEADS = 8
RET_DK = 64
RET_DV = 128
RET_QK = N_RET_HEADS * RET_DK
RET_V = N_RET_HEADS * RET_DV
RET_CHUNK = 128
ROPE_BASE = 10000.0
CONF_W = 512
CONF_K = 31
SC_W = 512
SC_K = 3
N_BRANCH = 3
D_FF = 2816
IN_COLS = 2 * RET_QK + 2 * RET_V + 2 * CONF_W + 3 * SC_W + N_BRANCH * D_MODEL
N_ADA = 9

kernel_name = "hybrid_retention_conformer_shortconv_step"


def rms_norm(x, g, eps=1e-6):
    xf = x.astype(jnp.float32)
    y = xf * lax.rsqrt(jnp.mean(xf * xf, axis=-1, keepdims=True) + eps)
    return (y * g.astype(jnp.float32)).astype(x.dtype)


def layer_norm(x, g, b, eps=1e-5):
    xf = x.astype(jnp.float32)
    mu = jnp.mean(xf, axis=-1, keepdims=True)
    var = jnp.mean(jnp.square(xf - mu), axis=-1, keepdims=True)
    y = (xf - mu) * lax.rsqrt(var + eps)
    return (y * g.astype(jnp.float32) + b.astype(jnp.float32)).astype(x.dtype)


def modulate(h, shift, scale):
    return h * (1.0 + scale) + shift


def swiglu(h, w1, w3, w2):
    return (jax.nn.silu(h @ w1) * (h @ w3)) @ w2


def rotary(x, pos):
    half = x.shape[-1] // 2
    freqs = ROPE_BASE ** (-jnp.arange(half, dtype=jnp.float32) / half)
    ang = pos[:, None] * freqs[None, :]
    cos = jnp.cos(ang)[None, :, None, :]
    sin = jnp.sin(ang)[None, :, None, :]
    x1, x2 = x[..., :half], x[..., half:]
    return jnp.concatenate([x1 * cos - x2 * sin, x1 * sin + x2 * cos], axis=-1)


def retention_log_gamma():
    return jnp.log(1.0 - jnp.exp2(-5.0 - jnp.arange(N_RET_HEADS, dtype=jnp.float32)))


def retention(q, k, v, S0, chunk):
    B, T, H, DK = q.shape
    DV = v.shape[-1]
    n = T // chunk
    lg = retention_log_gamma()
    idx = jnp.arange(chunk, dtype=jnp.float32)
    diff = idx[:, None] - idx[None, :]
    causal = diff >= 0
    decay = jnp.where(causal[None], jnp.exp(jnp.maximum(diff, 0.0)[None] * lg[:, None, None]), 0.0)
    q_decay = jnp.exp((idx + 1.0)[None, :] * lg[:, None]).T
    k_decay = jnp.exp((chunk - 1.0 - idx)[None, :] * lg[:, None])
    chunk_decay = jnp.exp(chunk * lg)

    def to_chunks(a):
        return a.reshape(B, n, chunk, H, a.shape[-1]).transpose(1, 0, 2, 3, 4)

    def step(S, inp):
        qc, kc, vc = inp
        scores = jnp.einsum('bihd,bjhd->bhij', qc, kc) * decay[None]
        o_intra = jnp.einsum('bhij,bjhe->bihe', scores, vc)
        o_cross = jnp.einsum('bihd,bhde->bihe', qc, S) * q_decay[None, :, :, None]
        S_new = S * chunk_decay[None, :, None, None] + jnp.einsum('bjhd,bjhe,hj->bhde', kc, vc, k_decay)
        return S_new, o_intra + o_cross

    S_fin, o = lax.scan(step, S0, (to_chunks(q), to_chunks(k), to_chunks(v)))
    o = o.transpose(1, 0, 2, 3, 4).reshape(B, T, H, DV)
    return o, S_fin


def head_group_norm(o, g, eps=1e-5):
    mu = jnp.mean(o, axis=-1, keepdims=True)
    var = jnp.mean(jnp.square(o - mu), axis=-1, keepdims=True)
    y = (o - mu) * lax.rsqrt(var + eps)
    B, T = o.shape[:2]
    return y.reshape(B, T, -1) * g.astype(jnp.float32)


def causal_dwconv(u, buf, w):
    K = w.shape[0]
    padded = jnp.concatenate([buf.astype(u.dtype), u], axis=1)
    out = lax.conv_general_dilated(
        padded, w[:, None, :].astype(u.dtype), window_strides=(1,), padding='VALID',
        dimension_numbers=('NWC', 'WIO', 'NWC'), feature_group_count=u.shape[-1])
    return out, padded[:, -(K - 1):]


def mixer(h, S0, conf_buf, sc_buf, pos0, w):
    B, T, _ = h.shape
    z = h @ w['w_in']
    sizes = [RET_QK, RET_QK, RET_V, RET_V, CONF_W, CONF_W, SC_W, SC_W, SC_W]
    splits = list(np.cumsum(sizes))
    q, k, v, g, ca, cb, sb, sc_c, sx, gl = jnp.split(z, splits, axis=-1)

    pos = pos0 + jnp.arange(T, dtype=jnp.float32)
    qf = rotary(q.astype(jnp.float32).reshape(B, T, N_RET_HEADS, RET_DK), pos) * (RET_DK ** -0.5)
    kf = rotary(k.astype(jnp.float32).reshape(B, T, N_RET_HEADS, RET_DK), pos)
    vf = v.astype(jnp.float32).reshape(B, T, N_RET_HEADS, RET_DV)
    chunk = RET_CHUNK if T % RET_CHUNK == 0 else T
    o, S_new = retention(qf, kf, vf, S0.astype(jnp.float32), chunk)
    o = head_group_norm(o, w['ret_gn_g']).astype(h.dtype)
    br_ret = (jax.nn.silu(g) * o) @ w['w_ret_out']

    u = ca * jax.nn.sigmoid(cb)
    cv, conf_new = causal_dwconv(u, conf_buf, w['conf_conv_w'])
    cv = layer_norm(cv + w['conf_conv_b'], w['conf_ln_g'], w['conf_ln_b'])
    br_conf = jax.nn.silu(cv) @ w['w_conf_out']

    us = sc_c * sx
    sv, sc_new = causal_dwconv(us, sc_buf, w['sc_conv_w'])
    br_sc = (sb * sv) @ w['w_sc_out']

    gates = jax.nn.sigmoid(gl + w['b_gate'])
    g_ret, g_conf, g_sc = jnp.split(gates, N_BRANCH, axis=-1)
    m = g_ret * br_ret + g_conf * br_conf + g_sc * br_sc
    return m @ w['w_o'], S_new.astype(S0.dtype), conf_new, sc_new


def trunk(x, c, state_ret, state_conf, state_sconv, pos0, p):
    rets, confs, scs = [], [], []
    for l in range(DEPTH):
        w = {name: arr[l] for name, arr in p.items() if name != 'g_final'}
        ada = jax.nn.silu(c) @ w['w_ada'] + w['b_ada']
        ada = ada[:, None, :]
        sh1, sc1, gt1, sh2, sc2, gt2, sh3, sc3, gt3 = jnp.split(ada, N_ADA, axis=-1)
        h = modulate(rms_norm(x, w['g_ffn1']), sh1, sc1)
        x = x + 0.5 * gt1 * swiglu(h, w['w1_a'], w['w3_a'], w['w2_a'])
        h = modulate(rms_norm(x, w['g_mix']), sh2, sc2)
        mo, S_new, conf_new, sc_new = mixer(h, state_ret[l], state_conf[l], state_sconv[l], pos0, w)
        x = x + gt2 * mo
        h = modulate(rms_norm(x, w['g_ffn2']), sh3, sc3)
        x = x + 0.5 * gt3 * swiglu(h, w['w1_b'], w['w3_b'], w['w2_b'])
        rets.append(S_new); confs.append(conf_new); scs.append(sc_new)
    y = rms_norm(x, p['g_final'])
    return y, jnp.stack(rets), jnp.stack(confs), jnp.stack(scs)


def setup_inputs(seed: int = 0) -> dict:
    key = jax.random.key(seed)
    ks = iter(jax.random.split(key, 64))
    f32 = jnp.float32

    def nrm(shape, scale):
        return jax.random.normal(next(ks), shape, f32) * scale

    def gain(shape):
        return 1.0 + 0.1 * jax.random.normal(next(ks), shape, f32)

    D, L = D_MODEL, DEPTH
    return {
        "x_prompt": nrm((BATCH, SEQ, D), 1.0),
        "x_sample": nrm((DEC_BATCH, DEC_SEQ, D), 1.0),
        "c_prompt": nrm((BATCH, D), 1.0),
        "c_sample": nrm((DEC_BATCH, D), 1.0),
        "state_ret": nrm((L, DEC_BATCH, N_RET_HEADS, RET_DK, RET_DV), 1.0),
        "state_conf": nrm((L, DEC_BATCH, CONF_K - 1, CONF_W), 1.0),
        "state_sconv": nrm((L, DEC_BATCH, SC_K - 1, SC_W), 1.0),
        "w_ada": nrm((L, D, N_ADA * D), 0.5 * D ** -0.5),
        "b_ada": nrm((L, N_ADA * D), 0.01),
        "g_ffn1": gain((L, D)),
        "w1_a": nrm((L, D, D_FF), D ** -0.5),
        "w3_a": nrm((L, D, D_FF), D ** -0.5),
        "w2_a": nrm((L, D_FF, D), D_FF ** -0.5),
        "g_mix": gain((L, D)),
        "w_in": nrm((L, D, IN_COLS), D ** -0.5),
        "b_gate": nrm((L, N_BRANCH * D), 0.01),
        "ret_gn_g": gain((L, RET_V)),
        "w_ret_out": nrm((L, RET_V, D), RET_V ** -0.5),
        "conf_conv_w": nrm((L, CONF_K, CONF_W), CONF_K ** -0.5),
        "conf_conv_b": nrm((L, CONF_W), 0.01),
        "conf_ln_g": gain((L, CONF_W)),
        "conf_ln_b": nrm((L, CONF_W), 0.01),
        "w_conf_out": nrm((L, CONF_W, D), CONF_W ** -0.5),
        "sc_conv_w": nrm((L, SC_K, SC_W), SC_K ** -0.5),
        "w_sc_out": nrm((L, SC_W, D), SC_W ** -0.5),
        "w_o": nrm((L, D, D), D ** -0.5),
        "g_ffn2": gain((L, D)),
        "w1_b": nrm((L, D, D_FF), D ** -0.5),
        "w3_b": nrm((L, D, D_FF), D ** -0.5),
        "w2_b": nrm((L, D_FF, D), D_FF ** -0.5),
        "g_final": gain((D,)),
    }


def reference(x_prompt, x_sample, c_prompt, c_sample, state_ret, state_conf, state_sconv,
              w_ada, b_ada, g_ffn1, w1_a, w3_a, w2_a, g_mix, w_in, b_gate, ret_gn_g, w_ret_out,
              conf_conv_w, conf_conv_b, conf_ln_g, conf_ln_b, w_conf_out, sc_conv_w, w_sc_out, w_o,
              g_ffn2, w1_b, w3_b, w2_b, g_final):
    p = dict(w_ada=w_ada, b_ada=b_ada, g_ffn1=g_ffn1, w1_a=w1_a, w3_a=w3_a, w2_a=w2_a,
             g_mix=g_mix, w_in=w_in, b_gate=b_gate, ret_gn_g=ret_gn_g, w_ret_out=w_ret_out,
             conf_conv_w=conf_conv_w, conf_conv_b=conf_conv_b, conf_ln_g=conf_ln_g,
             conf_ln_b=conf_ln_b, w_conf_out=w_conf_out, sc_conv_w=sc_conv_w, w_sc_out=w_sc_out,
             w_o=w_o, g_ffn2=g_ffn2, w1_b=w1_b, w3_b=w3_b, w2_b=w2_b, g_final=g_final)
    B = x_prompt.shape[0]
    ret0 = jnp.zeros((DEPTH, B, N_RET_HEADS, RET_DK, RET_DV), state_ret.dtype)
    conf0 = jnp.zeros((DEPTH, B, CONF_K - 1, CONF_W), x_prompt.dtype)
    sc0 = jnp.zeros((DEPTH, B, SC_K - 1, SC_W), x_prompt.dtype)
    y_prompt, ret_p, conf_p, sc_p = trunk(x_prompt, c_prompt, ret0, conf0, sc0, 0, p)
    y_sample, ret_s, conf_s, sc_s = trunk(x_sample, c_sample, state_ret, state_conf, state_sconv, PAST_LEN, p)
    return (y_prompt, y_sample, ret_p, ret_s, conf_p, conf_s, sc_p, sc_s)
```

```python
import functools

import jax
import jax.numpy as jnp
import numpy as np
from jax import lax
from jax.experimental import pallas as pl
from jax.experimental.pallas import tpu as pltpu

D_MODEL = 1024
DEPTH = 2
PAST_LEN = 16384
N_HEADS = 8
DK = 64
DV = 128
QK = N_HEADS * DK
RV = N_HEADS * DV
RET_CHUNK = 128
ROPE_BASE = 10000.0
CONF_W = 512
CONF_K = 31
SC_W = 512
SC_K = 3
D_FF = 2816
N_ADA = 9

_O_Q, _O_K, _O_V, _O_G = 0, QK, 2 * QK, 2 * QK + RV
_O_CA = 2 * QK + 2 * RV
_O_CB = _O_CA + CONF_W
_O_SB = _O_CB + CONF_W
_O_SC = _O_SB + SC_W
_O_SX = _O_SC + SC_W
_O_GL = _O_SX + SC_W
IN_COLS = _O_GL + 3 * D_MODEL

LANES = 128
SUBLANES = 8
CHUNK = 128
CONF_PAD = 32
SC_PAD = 8
VMEM_LIMIT = 56 * 1024 * 1024

BF16 = jnp.bfloat16
F32 = jnp.float32


def _dot(a, b):
    return jnp.dot(a, b, preferred_element_type=F32)


def _resident(shape, layer=None):
    if layer is None:
        nd = len(shape)
        return pl.BlockSpec(tuple(shape), lambda *_: (0,) * nd, pipeline_mode=pl.Buffered(1))
    nd = len(shape) - 1
    return pl.BlockSpec((None,) + tuple(shape[1:]), lambda *_: (layer,) + (0,) * nd,
                        pipeline_mode=pl.Buffered(1))


def _rms(x, g):
    ms = jnp.mean(x * x, axis=-1, keepdims=True)
    return x * lax.rsqrt(ms + 1e-6) * g


def _ada_kernel(c_ref, w_ref, b_ref, o_ref):
    s = jax.nn.silu(c_ref[...]).astype(BF16)
    o_ref[...] = _dot(s, w_ref[...].astype(BF16)) + b_ref[...]


def _ada(c_all, w_ada, b_ada):
    nb = c_all.shape[0]
    tn = D_MODEL
    return pl.pallas_call(
        _ada_kernel,
        grid=(DEPTH, N_ADA * D_MODEL // tn),
        in_specs=[
            pl.BlockSpec((nb, D_MODEL), lambda l, j: (0, 0)),
            pl.BlockSpec((None, D_MODEL, tn), lambda l, j: (l, 0, j)),
            pl.BlockSpec((None, 1, tn), lambda l, j: (l, 0, j)),
        ],
        out_specs=pl.BlockSpec((None, nb, tn), lambda l, j: (l, 0, j)),
        out_shape=jax.ShapeDtypeStruct((DEPTH, nb, N_ADA * D_MODEL), F32),
        compiler_params=pltpu.CompilerParams(dimension_semantics=("arbitrary", "arbitrary")),
        name="ada",
    )(c_all, w_ada, b_ada.reshape(DEPTH, 1, N_ADA * D_MODEL))


def _ffn_kernel(*refs, final):
    x_ref, sh_ref, sc_ref, gt_ref, g_ref, w1_ref, w3_ref, w2_ref = refs[:8]
    gf_ref = refs[8] if final else None
    o_ref = refs[-1]
    bs, tt, d = x_ref.shape
    x = x_ref[...]
    h = _rms(x, g_ref[...]) * (1.0 + sc_ref[...]) + sh_ref[...]
    hb = h.reshape(bs * tt, d).astype(BF16)
    half = D_FF // 2
    y = None
    for j in range(2):
        a = _dot(hb, w1_ref[:, j * half:(j + 1) * half])
        b = _dot(hb, w3_ref[:, j * half:(j + 1) * half])
        act = (jax.nn.silu(a) * b).astype(BF16)
        yj = _dot(act, w2_ref[j * half:(j + 1) * half, :])
        y = yj if y is None else y + yj
    out = x + (0.5 * gt_ref[...]) * y.reshape(bs, tt, d)
    if final:
        out = _rms(out, gf_ref[...])
    o_ref[...] = out


def _ffn(x, ada3, k0, g, w1, w3, w2, layer, bs, tt, g_final=None):
    b, t, d = x.shape
    final = g_final is not None

    def mod(k):
        return pl.BlockSpec((bs, 1, d), lambda i, j: (i, 0, k))

    in_specs = [
        pl.BlockSpec((bs, tt, d), lambda i, j: (i, j, 0)),
        mod(k0), mod(k0 + 1), mod(k0 + 2),
        _resident(g.shape, layer), _resident(w1.shape, layer),
        _resident(w3.shape, layer), _resident(w2.shape, layer),
    ]
    args = [x, ada3, ada3, ada3, g, w1, w3, w2]
    if final:
        in_specs.append(_resident(g_final.shape))
        args.append(g_final)
    return pl.pallas_call(
        functools.partial(_ffn_kernel, final=final),
        grid=(b // bs, t // tt),
        in_specs=in_specs,
        out_specs=pl.BlockSpec((bs, tt, d), lambda i, j: (i, j, 0)),
        out_shape=jax.ShapeDtypeStruct(x.shape, F32),
        compiler_params=pltpu.CompilerParams(
            dimension_semantics=("arbitrary", "arbitrary"), vmem_limit_bytes=VMEM_LIMIT),
        name="ffn_final" if final else "ffn",
    )(*args)


def _rotate(z, cos, sin_signed):
    lane = lax.broadcasted_iota(jnp.int32, z.shape, 1)
    first = (lane % DK) < (DK // 2)
    partner = jnp.where(first, pltpu.roll(z, QK - DK // 2, 1), pltpu.roll(z, DK // 2, 1))
    reps = QK // LANES
    cos4 = jnp.concatenate([cos] * reps, axis=1)
    sin4 = jnp.concatenate([sin_signed] * reps, axis=1)
    return z * cos4 + partner * sin4


def _mixer_kernel(*refs, bs, tt, has_state):
    (x_ref, sh_ref, scl_ref, gt_ref, gmix_ref, win_ref, bgate_ref, gng_ref, wret_ref,
     ccw_ref, ccb_ref, clg_ref, clb_ref, wconf_ref, scw_ref, wsc_ref, wo_ref,
     cosq_ref, sinq_ref, cosk_ref, sink_ref, dec_ref, qdec_ref, kdec_ref, cd_ref) = refs[:25]
    n_in = 25
    if has_state:
        ret0_ref, conf0_ref, sc0_ref = refs[25:28]
        n_in = 28
    xo_ref, ret_ref, confo_ref, sco_ref = refs[n_in:n_in + 4]
    q_s, kt_s, v_s, g_s, ro_s, oc_s, extc, exts = refs[n_in + 4:]
    m = bs * tt
    seg = tt if tt < CHUNK else CHUNK
    t_idx = pl.program_id(1)

    @pl.when(t_idx == 0)
    def _():
        if has_state:
            ret_ref[...] = ret0_ref[...]
            extc[:, CONF_PAD - (CONF_K - 1):CONF_PAD, :] = conf0_ref[...]
            exts[:, SC_PAD - (SC_K - 1):SC_PAD, :] = sc0_ref[...]
        else:
            ret_ref[...] = jnp.zeros(ret_ref.shape, F32)
            extc[:, 0:CONF_PAD, :] = jnp.zeros((bs, CONF_PAD, CONF_W), F32)
            exts[:, 0:SC_PAD, :] = jnp.zeros((bs, SC_PAD, SC_W), F32)

    x = x_ref[...]
    h = _rms(x, gmix_ref[...]) * (1.0 + scl_ref[...]) + sh_ref[...]
    hb = h.reshape(m, D_MODEL).astype(BF16)

    q_s[...] = _rotate(_dot(hb, win_ref[:, _O_Q:_O_Q + QK]), cosq_ref[...], sinq_ref[...])
    k_rot = _rotate(_dot(hb, win_ref[:, _O_K:_O_K + QK]), cosk_ref[...], sink_ref[...])
    v_s[...] = _dot(hb, win_ref[:, _O_V:_O_V + RV])
    g_s[...] = jax.nn.silu(_dot(hb, win_ref[:, _O_G:_O_G + RV]))

    lane = lax.broadcasted_iota(jnp.int32, (CHUNK, LANES), 1)
    low_half = lane < DK
    for c in range(m // CHUNK):
        r0 = c * CHUNK
        kt_s[...] = k_rot[r0:r0 + CHUNK, :].T
        for hd in range(N_HEADS):
            p = hd // 2
            q_pair = q_s[r0:r0 + CHUNK, p * LANES:(p + 1) * LANES]
            qm = jnp.where(low_half if hd % 2 == 0 else ~low_half, q_pair, 0.0)
            qmb = qm.astype(BF16)
            kt_pair = kt_s[p * LANES:(p + 1) * LANES, :].astype(BF16)
            kt_h = kt_s[hd * DK:(hd + 1) * DK, :]
            v_h = v_s[r0:r0 + CHUNK, hd * DV:(hd + 1) * DV]
            scores = _dot(qmb, kt_pair) * dec_ref[hd]
            o = _dot(scores.astype(BF16), v_h.astype(BF16))
            vd = (v_h * kdec_ref[hd]).astype(BF16)
            cdh = cd_ref[hd]
            if seg == CHUNK:
                sb = c * CHUNK // tt
                s_pair = ret_ref[sb, 2 * p:2 * p + 2].reshape(2 * DK, DV)
                o = o + _dot(qmb, s_pair.astype(BF16)) * qdec_ref[hd]
                ret_ref[sb, hd] = ret_ref[sb, hd] * cdh + _dot(kt_h.astype(BF16), vd)
            else:
                oc_s[...] = qm
                col = lax.broadcasted_iota(jnp.int32, (DK, CHUNK), 1)
                sb0 = c * (CHUNK // seg)

                def seg_body(b, carry, p=p, hd=hd, kt_h=kt_h, vd=vd, cdh=cdh, col=col, sb0=sb0):
                    r = pl.multiple_of(b * seg, seg)
                    qb = oc_s[pl.ds(r, seg), :].astype(BF16)
                    s_pair = ret_ref[sb0 + b, 2 * p:2 * p + 2].reshape(2 * DK, DV)
                    oc_s[pl.ds(r, seg), :] = _dot(qb, s_pair.astype(BF16))
                    ktm = jnp.where(col // seg == b, kt_h, 0.0).astype(BF16)
                    ret_ref[sb0 + b, hd] = ret_ref[sb0 + b, hd] * cdh + _dot(ktm, vd)
                    return carry

                lax.fori_loop(0, CHUNK // seg, seg_body, 0)
                o = o + oc_s[...] * qdec_ref[hd]
            mu = jnp.mean(o, axis=-1, keepdims=True)
            dev = o - mu
            var = jnp.mean(dev * dev, axis=-1, keepdims=True)
            y = dev * lax.rsqrt(var + 1e-5) * gng_ref[:, hd * DV:(hd + 1) * DV]
            gate = g_s[r0:r0 + CHUNK, hd * DV:(hd + 1) * DV]
            ro_s[r0:r0 + CHUNK, hd * DV:(hd + 1) * DV] = (gate * y).astype(BF16)

    gate_r = jax.nn.sigmoid(_dot(hb, win_ref[:, _O_GL:_O_GL + D_MODEL]) + bgate_ref[:, 0:D_MODEL])
    merged = gate_r * _dot(ro_s[...], wret_ref[...])

    u = _dot(hb, win_ref[:, _O_CA:_O_CA + CONF_W]) * jax.nn.sigmoid(
        _dot(hb, win_ref[:, _O_CB:_O_CB + CONF_W]))
    extc[:, CONF_PAD:CONF_PAD + tt, :] = u.reshape(bs, tt, CONF_W)
    base = CONF_PAD - (CONF_K - 1)
    acc = None
    for j in range(CONF_K):
        term = extc[:, base + j:base + j + tt, :] * ccw_ref[j:j + 1, :]
        acc = term if acc is None else acc + term
    cv = acc.reshape(m, CONF_W) + ccb_ref[...]
    mu = jnp.mean(cv, axis=-1, keepdims=True)
    dev = cv - mu
    var = jnp.mean(dev * dev, axis=-1, keepdims=True)
    cvn = dev * lax.rsqrt(var + 1e-5) * clg_ref[...] + clb_ref[...]
    br_conf = _dot(jax.nn.silu(cvn).astype(BF16), wconf_ref[...])
    gate_c = jax.nn.sigmoid(
        _dot(hb, win_ref[:, _O_GL + D_MODEL:_O_GL + 2 * D_MODEL]) + bgate_ref[:, D_MODEL:2 * D_MODEL])
    merged = merged + gate_c * br_conf
    confo_ref[...] = extc[:, tt + base:tt + CONF_PAD, :]
    if tt >= CONF_PAD:
        extc[:, 0:CONF_PAD, :] = extc[:, tt:tt + CONF_PAD, :]

    us = _dot(hb, win_ref[:, _O_SC:_O_SC + SC_W]) * _dot(hb, win_ref[:, _O_SX:_O_SX + SC_W])
    exts[:, SC_PAD:SC_PAD + tt, :] = us.reshape(bs, tt, SC_W)
    base = SC_PAD - (SC_K - 1)
    acc = None
    for j in range(SC_K):
        term = exts[:, base + j:base + j + tt, :] * scw_ref[j:j + 1, :]
        acc = term if acc is None else acc + term
    sv = acc.reshape(m, SC_W)
    br_sc = _dot((_dot(hb, win_ref[:, _O_SB:_O_SB + SC_W]) * sv).astype(BF16), wsc_ref[...])
    gate_s = jax.nn.sigmoid(
        _dot(hb, win_ref[:, _O_GL + 2 * D_MODEL:_O_GL + 3 * D_MODEL]) + bgate_ref[:, 2 * D_MODEL:])
    merged = merged + gate_s * br_sc
    sco_ref[...] = exts[:, tt + base:tt + SC_PAD, :]
    if tt >= SC_PAD:
        exts[:, 0:SC_PAD, :] = exts[:, tt:tt + SC_PAD, :]

    mo = _dot(merged.astype(BF16), wo_ref[...])
    xo_ref[...] = x + gt_ref[...] * mo.reshape(bs, tt, D_MODEL)


def _mixer(x, ada3, p, layer, bs, tt, tables, state=None):
    b, t, d = x.shape
    has_state = state is not None
    m = bs * tt
    cosq, sinq, cosk, sink, dec, qdec, kdec, cd = tables

    def mod(k):
        return pl.BlockSpec((bs, 1, d), lambda i, j: (i, 0, k))

    if cosq.shape[0] == m:
        rope = pl.BlockSpec((m, LANES), lambda i, j: (0, 0))
    else:
        rope = pl.BlockSpec((tt, LANES), lambda i, j: (j, 0))

    weights = [p["g_mix"], p["w_in"], p["b_gate"], p["ret_gn_g"], p["w_ret_out"],
               p["conf_conv_w"], p["conf_conv_b"], p["conf_ln_g"], p["conf_ln_b"],
               p["w_conf_out"], p["sc_conv_w"], p["w_sc_out"], p["w_o"]]
    in_specs = [pl.BlockSpec((bs, tt, d), lambda i, j: (i, j, 0)), mod(3), mod(4), mod(5)]
    in_specs += [_resident(w.shape, layer) for w in weights]
    in_specs += [rope] * 4 + [_resident(a.shape) for a in (dec, qdec, kdec, cd)]
    args = [x, ada3, ada3, ada3] + weights + [cosq, sinq, cosk, sink, dec, qdec, kdec, cd]
    if has_state:
        ret0, conf0, sc0 = state
        in_specs += [
            pl.BlockSpec((bs, N_HEADS, DK, DV), lambda i, j: (i, 0, 0, 0)),
            pl.BlockSpec((bs, CONF_K - 1, CONF_W), lambda i, j: (i, 0, 0)),
            pl.BlockSpec((bs, SC_K - 1, SC_W), lambda i, j: (i, 0, 0)),
        ]
        args += [ret0, conf0, sc0]
    out_specs = [
        pl.BlockSpec((bs, tt, d), lambda i, j: (i, j, 0)),
        pl.BlockSpec((bs, N_HEADS, DK, DV), lambda i, j: (i, 0, 0, 0)),
        pl.BlockSpec((bs, CONF_K - 1, CONF_W), lambda i, j: (i, 0, 0)),
        pl.BlockSpec((bs, SC_K - 1, SC_W), lambda i, j: (i, 0, 0)),
    ]
    out_shape = [
        jax.ShapeDtypeStruct(x.shape, F32),
        jax.ShapeDtypeStruct((b, N_HEADS, DK, DV), F32),
        jax.ShapeDtypeStruct((b, CONF_K - 1, CONF_W), F32),
        jax.ShapeDtypeStruct((b, SC_K - 1, SC_W), F32),
    ]
    scratch = [
        pltpu.VMEM((m, QK), F32),
        pltpu.VMEM((QK, CHUNK), F32),
        pltpu.VMEM((m, RV), F32),
        pltpu.VMEM((m, RV), F32),
        pltpu.VMEM((m, RV), BF16),
        pltpu.VMEM((CHUNK, LANES), F32),
        pltpu.VMEM((bs, CONF_PAD + tt, CONF_W), F32),
        pltpu.VMEM((bs, SC_PAD + tt, SC_W), F32),
    ]
    return pl.pallas_call(
        functools.partial(_mixer_kernel, bs=bs, tt=tt, has_state=has_state),
        grid=(b // bs, t // tt),
        in_specs=in_specs,
        out_specs=out_specs,
        out_shape=out_shape,
        scratch_shapes=scratch,
        compiler_params=pltpu.CompilerParams(
            dimension_semantics=("arbitrary", "arbitrary"), vmem_limit_bytes=VMEM_LIMIT),
        name="mixer_state" if has_state else "mixer",
    )(*args)


def _rope_tables(pos):
    half = DK // 2
    freqs = ROPE_BASE ** (-jnp.arange(half, dtype=F32) / half)
    ang = pos[:, None] * freqs[None, :]
    cos, sin = jnp.cos(ang), jnp.sin(ang)
    reps = LANES // DK
    cos_l = jnp.tile(jnp.concatenate([cos, cos], axis=1), (1, reps))
    sin_l = jnp.tile(jnp.concatenate([-sin, sin], axis=1), (1, reps))
    scale = DK ** -0.5
    return cos_l * scale, sin_l * scale, cos_l, sin_l


def _decay_tables(chunk):
    lg = jnp.log(1.0 - jnp.exp2(-5.0 - jnp.arange(N_HEADS, dtype=F32)))
    idx = jnp.arange(chunk, dtype=F32)
    diff = idx[:, None] - idx[None, :]
    causal = diff >= 0
    decay = jnp.where(causal[None], jnp.exp(jnp.maximum(diff, 0.0)[None] * lg[:, None, None]), 0.0)
    q_decay = jnp.exp((idx + 1.0)[None, :] * lg[:, None])
    k_decay = jnp.exp((chunk - 1.0 - idx)[None, :] * lg[:, None])
    chunk_decay = jnp.exp(chunk * lg)
    reps = CHUNK // chunk
    eye = jnp.eye(reps, dtype=F32)
    dec = jax.vmap(lambda dh: jnp.kron(eye, dh))(decay)
    qdec = jnp.broadcast_to(jnp.tile(q_decay, (1, reps))[:, :, None], (N_HEADS, CHUNK, DV))
    kdec = jnp.broadcast_to(jnp.tile(k_decay, (1, reps))[:, :, None], (N_HEADS, CHUNK, DV))
    cd = jnp.broadcast_to(chunk_decay[:, None, None], (N_HEADS, 1, DV))
    return dec, qdec, kdec, cd


def kernel(x_prompt, x_sample, c_prompt, c_sample, state_ret, state_conf, state_sconv,
           w_ada, b_ada, g_ffn1, w1_a, w3_a, w2_a, g_mix, w_in, b_gate, ret_gn_g, w_ret_out,
           conf_conv_w, conf_conv_b, conf_ln_g, conf_ln_b, w_conf_out, sc_conv_w, w_sc_out, w_o,
           g_ffn2, w1_b, w3_b, w2_b, g_final):
    nbp, seq, _ = x_prompt.shape
    nbs, dec_seq, _ = x_sample.shape

    def row(a):
        return a.reshape(DEPTH, 1, a.shape[-1])

    p = dict(
        g_mix=row(g_mix), w_in=w_in.astype(BF16), b_gate=row(b_gate), ret_gn_g=row(ret_gn_g),
        w_ret_out=w_ret_out.astype(BF16), conf_conv_w=conf_conv_w, conf_conv_b=row(conf_conv_b),
        conf_ln_g=row(conf_ln_g), conf_ln_b=row(conf_ln_b), w_conf_out=w_conf_out.astype(BF16),
        sc_conv_w=sc_conv_w, w_sc_out=w_sc_out.astype(BF16), w_o=w_o.astype(BF16))
    ffn_a = (row(g_ffn1), w1_a.astype(BF16), w3_a.astype(BF16), w2_a.astype(BF16))
    ffn_b = (row(g_ffn2), w1_b.astype(BF16), w3_b.astype(BF16), w2_b.astype(BF16))
    g_fin = g_final.reshape(1, D_MODEL)

    ada = _ada(jnp.concatenate([c_prompt, c_sample], axis=0), w_ada, b_ada)

    chunk_p = RET_CHUNK if seq % RET_CHUNK == 0 else seq
    tab_p = _rope_tables(jnp.arange(seq, dtype=F32)) + _decay_tables(chunk_p)
    chunk_s = RET_CHUNK if dec_seq % RET_CHUNK == 0 else dec_seq
    bs_mix = CHUNK // dec_seq
    rope_s = _rope_tables(PAST_LEN + jnp.arange(dec_seq, dtype=F32))
    tab_s = tuple(jnp.tile(a, (bs_mix, 1)) for a in rope_s) + _decay_tables(chunk_s)

    def trunk(x, ada_g, ffn_bs, ffn_tt, mix_bs, mix_tt, tables, states):
        ada3 = None
        rets, confs, scs = [], [], []
        for l in range(DEPTH):
            ada3 = ada_g[l][:, None, :]
            x = _ffn(x, ada3, 0, *ffn_a, l, ffn_bs, ffn_tt)
            st = None if states is None else tuple(s[l] for s in states)
            x, r, cf, sc = _mixer(x, ada3, p, l, mix_bs, mix_tt, tables, st)
            x = _ffn(x, ada3, 6, *ffn_b, l, ffn_bs, ffn_tt,
                     g_final=g_fin if l == DEPTH - 1 else None)
            rets.append(r)
            confs.append(cf)
            scs.append(sc)
        return x, jnp.stack(rets), jnp.stack(confs), jnp.stack(scs)

    y_p, ret_p, conf_p, sc_p = trunk(x_prompt, ada[:, :nbp], 1, 512, 1, 256, tab_p, None)
    y_s, ret_s, conf_s, sc_s = trunk(x_sample, ada[:, nbp:], 64, dec_seq, bs_mix, dec_seq, tab_s,
                                     (state_ret, state_conf, state_sconv))
    return (y_p, y_s, ret_p, ret_s, conf_p, conf_s, sc_p, sc_s)
```

```python
import functools

import jax
import jax.numpy as jnp
from jax import lax
from jax.experimental import pallas as pl
from jax.experimental.pallas import tpu as pltpu

D_MODEL = 1024
DEPTH = 2
PAST_LEN = 16384
N_HEADS = 8
DK = 64
DV = 128
QK = N_HEADS * DK
RV = N_HEADS * DV
RET_CHUNK = 128
ROPE_BASE = 10000.0
CONF_W = 512
CONF_K = 31
SC_W = 512
SC_K = 3
D_FF = 2816
N_ADA = 9

_O_Q, _O_K, _O_V, _O_G = 0, QK, 2 * QK, 2 * QK + RV
_O_CA = 2 * QK + 2 * RV
_O_CB = _O_CA + CONF_W
_O_SB = _O_CB + CONF_W
_O_SC = _O_SB + SC_W
_O_SX = _O_SC + SC_W
_O_GL = _O_SX + SC_W
IN_COLS = _O_GL + 3 * D_MODEL

LANES = 128
CHUNK = 128
CONF_PAD = 32
SC_PAD = 8
VMEM_LIMIT = 56 * 1024 * 1024

BF16 = jnp.bfloat16
F32 = jnp.float32


def _dot(a, b):
    return jnp.dot(a, b, preferred_element_type=F32)


def _resident(shape, layer=None):
    if layer is None:
        nd = len(shape)
        return pl.BlockSpec(tuple(shape), lambda *_: (0,) * nd, pipeline_mode=pl.Buffered(1))
    nd = len(shape) - 1
    return pl.BlockSpec((None,) + tuple(shape[1:]), lambda *_: (layer,) + (0,) * nd,
                        pipeline_mode=pl.Buffered(1))


def _rms(x, g):
    ms = jnp.mean(x * x, axis=-1, keepdims=True)
    return x * lax.rsqrt(ms + 1e-6) * g


def _ada_kernel(c_ref, w_ref, b_ref, o_ref):
    s = jax.nn.silu(c_ref[...]).astype(BF16)
    o_ref[...] = _dot(s, w_ref[...].astype(BF16)) + b_ref[...]


def _ada(c_all, w_ada, b_ada):
    nb = c_all.shape[0]
    tn = D_MODEL
    return pl.pallas_call(
        _ada_kernel,
        grid=(DEPTH, N_ADA * D_MODEL // tn),
        in_specs=[
            pl.BlockSpec((nb, D_MODEL), lambda l, j: (0, 0)),
            pl.BlockSpec((None, D_MODEL, tn), lambda l, j: (l, 0, j)),
            pl.BlockSpec((None, 1, tn), lambda l, j: (l, 0, j)),
        ],
        out_specs=pl.BlockSpec((None, nb, tn), lambda l, j: (l, 0, j)),
        out_shape=jax.ShapeDtypeStruct((DEPTH, nb, N_ADA * D_MODEL), F32),
        compiler_params=pltpu.CompilerParams(dimension_semantics=("arbitrary", "arbitrary")),
        name="ada",
    )(c_all, w_ada, b_ada.reshape(DEPTH, 1, N_ADA * D_MODEL))


def _ffn_kernel(*refs, final):
    x_ref, sh_ref, sc_ref, gt_ref, g_ref, w1_ref, w3_ref, w2_ref = refs[:8]
    gf_ref = refs[8] if final else None
    o_ref = refs[-1]
    bs, tt, d = x_ref.shape
    x = x_ref[...]
    h = _rms(x, g_ref[...]) * (1.0 + sc_ref[...]) + sh_ref[...]
    hb = h.reshape(bs * tt, d).astype(BF16)
    half = D_FF // 2
    y = None
    for j in range(2):
        a = _dot(hb, w1_ref[:, j * half:(j + 1) * half])
        b = _dot(hb, w3_ref[:, j * half:(j + 1) * half])
        act = (jax.nn.silu(a) * b).astype(BF16)
        yj = _dot(act, w2_ref[j * half:(j + 1) * half, :])
        y = yj if y is None else y + yj
    out = x + (0.5 * gt_ref[...]) * y.reshape(bs, tt, d)
    if final:
        out = _rms(out, gf_ref[...])
    o_ref[...] = out


def _ffn(x, ada3, k0, g, w1, w3, w2, layer, bs, tt, g_final=None):
    b, t, d = x.shape
    final = g_final is not None

    def mod(k):
        return pl.BlockSpec((bs, 1, d), lambda i, j: (i, 0, k))

    in_specs = [
        pl.BlockSpec((bs, tt, d), lambda i, j: (i, j, 0)),
        mod(k0), mod(k0 + 1), mod(k0 + 2),
        _resident(g.shape, layer), _resident(w1.shape, layer),
        _resident(w3.shape, layer), _resident(w2.shape, layer),
    ]
    args = [x, ada3, ada3, ada3, g, w1, w3, w2]
    if final:
        in_specs.append(_resident(g_final.shape))
        args.append(g_final)
    return pl.pallas_call(
        functools.partial(_ffn_kernel, final=final),
        grid=(b // bs, t // tt),
        in_specs=in_specs,
        out_specs=pl.BlockSpec((bs, tt, d), lambda i, j: (i, j, 0)),
        out_shape=jax.ShapeDtypeStruct(x.shape, F32),
        compiler_params=pltpu.CompilerParams(
            dimension_semantics=("arbitrary", "arbitrary"), vmem_limit_bytes=VMEM_LIMIT),
        name="ffn_final" if final else "ffn",
    )(*args)


def _rotate(z, cos, sin_signed):
    lane = lax.broadcasted_iota(jnp.int32, z.shape, 1)
    first = (lane % DK) < (DK // 2)
    partner = jnp.where(first, pltpu.roll(z, QK - DK // 2, 1), pltpu.roll(z, DK // 2, 1))
    reps = QK // LANES
    cos4 = jnp.concatenate([cos] * reps, axis=1)
    sin4 = jnp.concatenate([sin_signed] * reps, axis=1)
    return z * cos4 + partner * sin4


def _mixer_kernel(*refs, bs, tt, has_state, n_alias):
    (x_ref, sh_ref, scl_ref, gt_ref, gmix_ref, win_ref, bgate_ref, gng_ref, wret_ref,
     ccw_ref, ccb_ref, clg_ref, clb_ref, wconf_ref, scw_ref, wsc_ref, wo_ref,
     cosq_ref, sinq_ref, cosk_ref, sink_ref, dec_ref, qdec_ref, kdec_ref, cd_ref) = refs[:25]
    n_in = 25
    if has_state:
        ret0_ref, conf0_ref, sc0_ref = refs[25:28]
        n_in = 28
    n_in += n_alias
    xo_ref, ret_ref, confo_ref, sco_ref = refs[n_in:n_in + 4]
    q_s, kt_s, v_s, g_s, ro_s, sb_s, extc, exts = refs[n_in + 4:]
    m = bs * tt
    seg = tt if tt < CHUNK else CHUNK
    nseg = CHUNK // seg
    n_slab = CONF_W // LANES
    cbase = CONF_PAD - (CONF_K - 1)
    sbase = SC_PAD - (SC_K - 1)
    t_idx = pl.program_id(1)

    def lanes(s):
        return slice(s * LANES, (s + 1) * LANES)

    @pl.when(t_idx == 0)
    def _():
        if has_state:
            ret_ref[...] = ret0_ref[...]
            for s in range(n_slab):
                extc[:, s, cbase:CONF_PAD, :] = conf0_ref[:, :, lanes(s)]
                exts[:, s, sbase:SC_PAD, :] = sc0_ref[:, :, lanes(s)]
        else:
            ret_ref[...] = jnp.zeros(ret_ref.shape, F32)
            extc[:, :, 0:CONF_PAD, :] = jnp.zeros((bs, n_slab, CONF_PAD, LANES), F32)
            exts[:, :, 0:SC_PAD, :] = jnp.zeros((bs, n_slab, SC_PAD, LANES), F32)

    x = x_ref[...]
    h = _rms(x, gmix_ref[...]) * (1.0 + scl_ref[...]) + sh_ref[...]
    hb = h.reshape(m, D_MODEL).astype(BF16)

    def proj(off, width):
        return _dot(hb, win_ref[:, off:off + width])

    q_s[...] = _rotate(proj(_O_Q, QK), cosq_ref[...], sinq_ref[...])
    k_rot = _rotate(proj(_O_K, QK), cosk_ref[...], sink_ref[...])
    v_s[...] = proj(_O_V, RV)
    g_s[...] = jax.nn.silu(proj(_O_G, RV))
    u = proj(_O_CA, CONF_W) * jax.nn.sigmoid(proj(_O_CB, CONF_W))
    us = proj(_O_SC, SC_W) * proj(_O_SX, SC_W)
    sb_s[...] = proj(_O_SB, SC_W)
    for s in range(n_slab):
        extc[:, s, CONF_PAD:CONF_PAD + tt, :] = u[:, lanes(s)].reshape(bs, tt, LANES)
        exts[:, s, SC_PAD:SC_PAD + tt, :] = us[:, lanes(s)].reshape(bs, tt, LANES)

    gates = [None] * 3
    cv_slab = [None] * n_slab
    sv_slab = [None] * n_slab

    def gate_task(i):
        def run():
            gates[i] = jax.nn.sigmoid(proj(_O_GL + i * D_MODEL, D_MODEL)
                                      + bgate_ref[:, i * D_MODEL:(i + 1) * D_MODEL])
        return run

    def conv_task(s):
        def run():
            acc = None
            for j in range(CONF_K):
                term = extc[:, s, cbase + j:cbase + j + tt, :] * ccw_ref[j:j + 1, lanes(s)]
                acc = term if acc is None else acc + term
            cv_slab[s] = acc.reshape(m, LANES)
        return run

    def sconv_task():
        for s in range(n_slab):
            acc = None
            for j in range(SC_K):
                term = exts[:, s, sbase + j:sbase + j + tt, :] * scw_ref[j:j + 1, lanes(s)]
                acc = term if acc is None else acc + term
            sv_slab[s] = acc.reshape(m, LANES)

    fillers = [gate_task(0), conv_task(0), conv_task(1), gate_task(1), conv_task(2), conv_task(3),
               gate_task(2), sconv_task]
    n_fill = len(fillers)

    lane = lax.broadcasted_iota(jnp.int32, (CHUNK, LANES), 1)
    low_half = lane < DK
    row_seq = lax.broadcasted_iota(jnp.int32, (CHUNK, LANES), 0) // seg
    col_seq = lax.broadcasted_iota(jnp.int32, (DK, CHUNK), 1) // seg
    n_iter = (m // CHUNK) * N_HEADS
    it = 0
    for c in range(m // CHUNK):
        r0 = c * CHUNK
        s0 = c * CHUNK // tt
        kt_s[...] = k_rot[r0:r0 + CHUNK, :].T
        for hd in range(N_HEADS):
            p = hd // 2
            q_pair = q_s[r0:r0 + CHUNK, p * LANES:(p + 1) * LANES]
            qm = jnp.where(low_half if hd % 2 == 0 else ~low_half, q_pair, 0.0)
            qmb = qm.astype(BF16)
            kt_pair = kt_s[p * LANES:(p + 1) * LANES, :].astype(BF16)
            kt_h = kt_s[hd * DK:(hd + 1) * DK, :]
            v_h = v_s[r0:r0 + CHUNK, hd * DV:(hd + 1) * DV]
            scores = _dot(qmb, kt_pair) * dec_ref[hd]
            o = _dot(scores.astype(BF16), v_h.astype(BF16))
            vd = (v_h * kdec_ref[hd]).astype(BF16)
            if nseg == 1:
                s_pair = ret_ref[s0, 2 * p:2 * p + 2].reshape(2 * DK, DV)
                o = o + _dot(qmb, s_pair.astype(BF16)) * qdec_ref[hd]
                ret_ref[s0, hd] = ret_ref[s0, hd] * cd_ref[hd] + _dot(kt_h.astype(BF16), vd)
            else:
                q_exp = jnp.concatenate(
                    [jnp.where(row_seq == b, qm, 0.0) for b in range(nseg)], axis=1).astype(BF16)
                s_all = ret_ref[s0:s0 + nseg, 2 * p:2 * p + 2].reshape(nseg * 2 * DK, DV)
                o = o + _dot(q_exp, s_all.astype(BF16)) * qdec_ref[hd]
                kt_exp = jnp.concatenate(
                    [jnp.where(col_seq == b, kt_h, 0.0) for b in range(nseg)], axis=0).astype(BF16)
                upd = _dot(kt_exp, vd).reshape(nseg, DK, DV)
                ret_ref[s0:s0 + nseg, hd] = ret_ref[s0:s0 + nseg, hd] * cd_ref[hd] + upd
            mu = jnp.mean(o, axis=-1, keepdims=True)
            dev = o - mu
            var = jnp.mean(dev * dev, axis=-1, keepdims=True)
            y = dev * lax.rsqrt(var + 1e-5) * gng_ref[:, hd * DV:(hd + 1) * DV]
            gate = g_s[r0:r0 + CHUNK, hd * DV:(hd + 1) * DV]
            ro_s[r0:r0 + CHUNK, hd * DV:(hd + 1) * DV] = (gate * y).astype(BF16)
            it += 1
            while fillers and (n_fill - len(fillers)) * n_iter < it * n_fill:
                fillers.pop(0)()
    while fillers:
        fillers.pop(0)()

    merged = gates[0] * _dot(ro_s[...], wret_ref[...])

    cv = jnp.concatenate(cv_slab, axis=1) + ccb_ref[...]
    mu = jnp.mean(cv, axis=-1, keepdims=True)
    dev = cv - mu
    var = jnp.mean(dev * dev, axis=-1, keepdims=True)
    cvn = dev * lax.rsqrt(var + 1e-5) * clg_ref[...] + clb_ref[...]
    merged = merged + gates[1] * _dot(jax.nn.silu(cvn).astype(BF16), wconf_ref[...])

    sv = jnp.concatenate(sv_slab, axis=1)
    merged = merged + gates[2] * _dot((sb_s[...] * sv).astype(BF16), wsc_ref[...])

    for s in range(n_slab):
        confo_ref[:, :, lanes(s)] = extc[:, s, tt + cbase:tt + CONF_PAD, :]
        sco_ref[:, :, lanes(s)] = exts[:, s, tt + sbase:tt + SC_PAD, :]
        if tt >= CONF_PAD:
            extc[:, s, 0:CONF_PAD, :] = extc[:, s, tt:tt + CONF_PAD, :]
            exts[:, s, 0:SC_PAD, :] = exts[:, s, tt:tt + SC_PAD, :]

    mo = _dot(merged.astype(BF16), wo_ref[...])
    xo_ref[...] = x + gt_ref[...] * mo.reshape(bs, tt, D_MODEL)


def _mixer(x, ada3, p, layer, bs, tt, tables, state=None, prev=None):
    b, t, d = x.shape
    has_state = state is not None
    m = bs * tt
    cosq, sinq, cosk, sink, dec, qdec, kdec, cd = tables

    def mod(k):
        return pl.BlockSpec((bs, 1, d), lambda i, j: (i, 0, k))

    if cosq.shape[0] == m:
        rope = pl.BlockSpec((m, LANES), lambda i, j: (0, 0))
    else:
        rope = pl.BlockSpec((tt, LANES), lambda i, j: (j, 0))

    state_specs = [
        pl.BlockSpec((None, bs, N_HEADS, DK, DV), lambda i, j: (layer, i, 0, 0, 0)),
        pl.BlockSpec((None, bs, CONF_K - 1, CONF_W), lambda i, j: (layer, i, 0, 0)),
        pl.BlockSpec((None, bs, SC_K - 1, SC_W), lambda i, j: (layer, i, 0, 0)),
    ]
    weights = [p["g_mix"], p["w_in"], p["b_gate"], p["ret_gn_g"], p["w_ret_out"],
               p["conf_conv_w"], p["conf_conv_b"], p["conf_ln_g"], p["conf_ln_b"],
               p["w_conf_out"], p["sc_conv_w"], p["w_sc_out"], p["w_o"]]
    in_specs = [pl.BlockSpec((bs, tt, d), lambda i, j: (i, j, 0)), mod(3), mod(4), mod(5)]
    in_specs += [_resident(w.shape, layer) for w in weights]
    in_specs += [rope] * 4 + [_resident(a.shape) for a in (dec, qdec, kdec, cd)]
    args = [x, ada3, ada3, ada3] + weights + [cosq, sinq, cosk, sink, dec, qdec, kdec, cd]
    if has_state:
        in_specs += state_specs
        args += list(state)
    aliases = {}
    if prev is not None:
        for k, a in enumerate(prev):
            aliases[len(args)] = 1 + k
            in_specs.append(pl.BlockSpec(memory_space=pl.ANY))
            args.append(a)
    out_specs = [pl.BlockSpec((bs, tt, d), lambda i, j: (i, j, 0))] + state_specs
    out_shape = [
        jax.ShapeDtypeStruct(x.shape, F32),
        jax.ShapeDtypeStruct((DEPTH, b, N_HEADS, DK, DV), F32),
        jax.ShapeDtypeStruct((DEPTH, b, CONF_K - 1, CONF_W), F32),
        jax.ShapeDtypeStruct((DEPTH, b, SC_K - 1, SC_W), F32),
    ]
    scratch = [
        pltpu.VMEM((m, QK), F32),
        pltpu.VMEM((QK, CHUNK), F32),
        pltpu.VMEM((m, RV), F32),
        pltpu.VMEM((m, RV), F32),
        pltpu.VMEM((m, RV), BF16),
        pltpu.VMEM((m, SC_W), F32),
        pltpu.VMEM((bs, CONF_W // LANES, CONF_PAD + tt, LANES), F32),
        pltpu.VMEM((bs, SC_W // LANES, SC_PAD + tt, LANES), F32),
    ]
    return pl.pallas_call(
        functools.partial(_mixer_kernel, bs=bs, tt=tt, has_state=has_state,
                          n_alias=len(aliases)),
        grid=(b // bs, t // tt),
        in_specs=in_specs,
        out_specs=out_specs,
        out_shape=out_shape,
        scratch_shapes=scratch,
        input_output_aliases=aliases,
        compiler_params=pltpu.CompilerParams(
            dimension_semantics=("arbitrary", "arbitrary"), vmem_limit_bytes=VMEM_LIMIT),
        name="mixer_state" if has_state else "mixer",
    )(*args)


def _rope_tables(pos):
    half = DK // 2
    freqs = ROPE_BASE ** (-jnp.arange(half, dtype=F32) / half)
    ang = pos[:, None] * freqs[None, :]
    cos, sin = jnp.cos(ang), jnp.sin(ang)
    reps = LANES // DK
    cos_l = jnp.tile(jnp.concatenate([cos, cos], axis=1), (1, reps))
    sin_l = jnp.tile(jnp.concatenate([-sin, sin], axis=1), (1, reps))
    scale = DK ** -0.5
    return cos_l * scale, sin_l * scale, cos_l, sin_l


def _decay_tables(chunk):
    lg = jnp.log(1.0 - jnp.exp2(-5.0 - jnp.arange(N_HEADS, dtype=F32)))
    row = jnp.arange(CHUNK)
    idx = (row % chunk).astype(F32)
    seq = row // chunk
    diff = idx[:, None] - idx[None, :]
    keep = (seq[:, None] == seq[None, :]) & (diff >= 0)
    dec = jnp.where(keep[None], jnp.exp(jnp.maximum(diff, 0.0)[None] * lg[:, None, None]), 0.0)
    q_decay = jnp.exp((idx + 1.0)[None, :] * lg[:, None])
    k_decay = jnp.exp((chunk - 1.0 - idx)[None, :] * lg[:, None])
    chunk_decay = jnp.exp(chunk * lg)
    qdec = jnp.broadcast_to(q_decay[:, :, None], (N_HEADS, CHUNK, DV))
    kdec = jnp.broadcast_to(k_decay[:, :, None], (N_HEADS, CHUNK, DV))
    cd = jnp.broadcast_to(chunk_decay[:, None, None], (N_HEADS, 1, DV))
    return dec, qdec, kdec, cd


def kernel(x_prompt, x_sample, c_prompt, c_sample, state_ret, state_conf, state_sconv,
           w_ada, b_ada, g_ffn1, w1_a, w3_a, w2_a, g_mix, w_in, b_gate, ret_gn_g, w_ret_out,
           conf_conv_w, conf_conv_b, conf_ln_g, conf_ln_b, w_conf_out, sc_conv_w, w_sc_out, w_o,
           g_ffn2, w1_b, w3_b, w2_b, g_final):
    nbp, seq, _ = x_prompt.shape
    nbs, dec_seq, _ = x_sample.shape

    def row(a):
        return a.reshape(DEPTH, 1, a.shape[-1])

    p = dict(
        g_mix=row(g_mix), w_in=w_in.astype(BF16), b_gate=row(b_gate), ret_gn_g=row(ret_gn_g),
        w_ret_out=w_ret_out.astype(BF16), conf_conv_w=conf_conv_w, conf_conv_b=row(conf_conv_b),
        conf_ln_g=row(conf_ln_g), conf_ln_b=row(conf_ln_b), w_conf_out=w_conf_out.astype(BF16),
        sc_conv_w=sc_conv_w, w_sc_out=w_sc_out.astype(BF16), w_o=w_o.astype(BF16))
    ffn_a = (row(g_ffn1), w1_a.astype(BF16), w3_a.astype(BF16), w2_a.astype(BF16))
    ffn_b = (row(g_ffn2), w1_b.astype(BF16), w3_b.astype(BF16), w2_b.astype(BF16))
    g_fin = g_final.reshape(1, D_MODEL)

    ada = _ada(jnp.concatenate([c_prompt, c_sample], axis=0), w_ada, b_ada)

    chunk_p = RET_CHUNK if seq % RET_CHUNK == 0 else seq
    tab_p = _rope_tables(jnp.arange(seq, dtype=F32)) + _decay_tables(chunk_p)
    chunk_s = RET_CHUNK if dec_seq % RET_CHUNK == 0 else dec_seq
    bs_mix = CHUNK // dec_seq
    rope_s = _rope_tables(PAST_LEN + jnp.arange(dec_seq, dtype=F32))
    tab_s = tuple(jnp.tile(a, (bs_mix, 1)) for a in rope_s) + _decay_tables(chunk_s)

    def trunk(x, ada_g, ffn_bs, ffn_tt, mix_bs, mix_tt, tables, states):
        new_states = None
        for l in range(DEPTH):
            ada3 = ada_g[l][:, None, :]
            x = _ffn(x, ada3, 0, *ffn_a, l, ffn_bs, ffn_tt)
            x, *new_states = _mixer(x, ada3, p, l, mix_bs, mix_tt, tables, states, new_states)
            x = _ffn(x, ada3, 6, *ffn_b, l, ffn_bs, ffn_tt,
                     g_final=g_fin if l == DEPTH - 1 else None)
        return (x, *new_states)

    y_p, ret_p, conf_p, sc_p = trunk(x_prompt, ada[:, :nbp], 1, 512, 1, 256, tab_p, None)
    y_s, ret_s, conf_s, sc_s = trunk(x_sample, ada[:, nbp:], 64, dec_seq, bs_mix, dec_seq, tab_s,
                                     (state_ret, state_conf, state_sconv))
    return (y_p, y_s, ret_p, ret_s, conf_p, conf_s, sc_p, sc_s)
```

```python
import functools

import jax
import jax.numpy as jnp
from jax import lax
from jax.experimental import pallas as pl
from jax.experimental.pallas import tpu as pltpu

D_MODEL = 1024
DEPTH = 2
PAST_LEN = 16384
N_HEADS = 8
DK = 64
DV = 128
QK = N_HEADS * DK
RV = N_HEADS * DV
RET_CHUNK = 128
ROPE_BASE = 10000.0
CONF_W = 512
CONF_K = 31
SC_W = 512
SC_K = 3
D_FF = 2816
N_ADA = 9

_O_Q, _O_K, _O_V, _O_G = 0, QK, 2 * QK, 2 * QK + RV
_O_CA = 2 * QK + 2 * RV
_O_CB = _O_CA + CONF_W
_O_SB = _O_CB + CONF_W
_O_SC = _O_SB + SC_W
_O_SX = _O_SC + SC_W
_O_GL = _O_SX + SC_W
IN_COLS = _O_GL + 3 * D_MODEL

LANES = 128
CHUNK = 128
CONF_PAD = 32
SC_PAD = 8
MXU_TILE = 256
FF_SPLITS = ((0, 6 * MXU_TILE), (6 * MXU_TILE, D_FF))
VMEM_LIMIT = 56 * 1024 * 1024

BF16 = jnp.bfloat16
F32 = jnp.float32


def _dot(a, b):
    return jnp.dot(a, b, preferred_element_type=F32)


def _resident(shape, layer=None):
    if layer is None:
        nd = len(shape)
        return pl.BlockSpec(tuple(shape), lambda *_: (0,) * nd, pipeline_mode=pl.Buffered(1))
    nd = len(shape) - 1
    return pl.BlockSpec((None,) + tuple(shape[1:]), lambda *_: (layer,) + (0,) * nd,
                        pipeline_mode=pl.Buffered(1))


def _rms(x, g):
    ms = jnp.mean(x * x, axis=-1, keepdims=True)
    return x * lax.rsqrt(ms + 1e-6) * g


def _ada_kernel(c_ref, w_ref, b_ref, o_ref):
    s = jax.nn.silu(c_ref[...]).astype(BF16)
    o_ref[...] = _dot(s, w_ref[...].astype(BF16)) + b_ref[...]


def _ada(c_all, w_ada, b_ada):
    nb = c_all.shape[0]
    tn = D_MODEL
    return pl.pallas_call(
        _ada_kernel,
        grid=(DEPTH, N_ADA * D_MODEL // tn),
        in_specs=[
            pl.BlockSpec((nb, D_MODEL), lambda l, j: (0, 0)),
            pl.BlockSpec((None, D_MODEL, tn), lambda l, j: (l, 0, j)),
            pl.BlockSpec((None, 1, tn), lambda l, j: (l, 0, j)),
        ],
        out_specs=pl.BlockSpec((None, nb, tn), lambda l, j: (l, 0, j)),
        out_shape=jax.ShapeDtypeStruct((DEPTH, nb, N_ADA * D_MODEL), F32),
        compiler_params=pltpu.CompilerParams(dimension_semantics=("arbitrary", "arbitrary")),
        name="ada",
    )(c_all, w_ada, b_ada.reshape(DEPTH, 1, N_ADA * D_MODEL))


def _ffn_kernel(*refs, final):
    x_ref, sh_ref, sc_ref, gt_ref, g_ref, w1_ref, w3_ref, w2_ref = refs[:8]
    gf_ref = refs[8] if final else None
    o_ref = refs[-1]
    bs, tt, d = x_ref.shape
    x = x_ref[...]
    h = _rms(x, g_ref[...]) * (1.0 + sc_ref[...]) + sh_ref[...]
    hb = h.reshape(bs * tt, d).astype(BF16)
    y = None
    for lo, hi in FF_SPLITS:
        a = _dot(hb, w1_ref[:, lo:hi])
        b = _dot(hb, w3_ref[:, lo:hi])
        act = (jax.nn.silu(a) * b).astype(BF16)
        yj = _dot(act, w2_ref[lo:hi, :])
        y = yj if y is None else y + yj
    out = x + (0.5 * gt_ref[...]) * y.reshape(bs, tt, d)
    if final:
        out = _rms(out, gf_ref[...])
    o_ref[...] = out


def _ffn(x, ada3, k0, g, w1, w3, w2, layer, bs, tt, g_final=None):
    b, t, d = x.shape
    final = g_final is not None

    def mod(k):
        return pl.BlockSpec((bs, 1, d), lambda i, j: (i, 0, k))

    in_specs = [
        pl.BlockSpec((bs, tt, d), lambda i, j: (i, j, 0)),
        mod(k0), mod(k0 + 1), mod(k0 + 2),
        _resident(g.shape, layer), _resident(w1.shape, layer),
        _resident(w3.shape, layer), _resident(w2.shape, layer),
    ]
    args = [x, ada3, ada3, ada3, g, w1, w3, w2]
    if final:
        in_specs.append(_resident(g_final.shape))
        args.append(g_final)
    return pl.pallas_call(
        functools.partial(_ffn_kernel, final=final),
        grid=(b // bs, t // tt),
        in_specs=in_specs,
        out_specs=pl.BlockSpec((bs, tt, d), lambda i, j: (i, j, 0)),
        out_shape=jax.ShapeDtypeStruct(x.shape, F32),
        compiler_params=pltpu.CompilerParams(
            dimension_semantics=("arbitrary", "arbitrary"), vmem_limit_bytes=VMEM_LIMIT),
        name="ffn_final" if final else "ffn",
    )(*args)


def _rotate(z, cos, sin_signed):
    lane = lax.broadcasted_iota(jnp.int32, z.shape, 1)
    first = (lane % DK) < (DK // 2)
    partner = jnp.where(first, pltpu.roll(z, QK - DK // 2, 1), pltpu.roll(z, DK // 2, 1))
    reps = QK // LANES
    cos4 = jnp.concatenate([cos] * reps, axis=1)
    sin4 = jnp.concatenate([sin_signed] * reps, axis=1)
    return z * cos4 + partner * sin4


def _mixer_kernel(*refs, bs, tt, has_state, n_alias):
    (x_ref, sh_ref, scl_ref, gt_ref, gmix_ref, win_ref, bgate_ref, gng_ref, wret_ref,
     ccw_ref, ccb_ref, clg_ref, clb_ref, wconf_ref, scw_ref, wsc_ref, wo_ref,
     cosq_ref, sinq_ref, cosk_ref, sink_ref, dec_ref, qdec_ref, kdec_ref, cd_ref) = refs[:25]
    n_in = 25
    if has_state:
        ret0_ref, conf0_ref, sc0_ref = refs[25:28]
        n_in = 28
    n_in += n_alias
    xo_ref, ret_ref, confo_ref, sco_ref = refs[n_in:n_in + 4]
    q_s, kt_s, v_s, g_s, ro_s, sb_s, extc, exts = refs[n_in + 4:]
    m = bs * tt
    seg = tt if tt < CHUNK else CHUNK
    nseg = CHUNK // seg
    n_slab = CONF_W // LANES
    cbase = CONF_PAD - (CONF_K - 1)
    sbase = SC_PAD - (SC_K - 1)
    t_idx = pl.program_id(1)

    def lanes(s):
        return slice(s * LANES, (s + 1) * LANES)

    @pl.when(t_idx == 0)
    def _():
        if has_state:
            ret_ref[...] = ret0_ref[...]
            for s in range(n_slab):
                extc[:, s, cbase:CONF_PAD, :] = conf0_ref[:, :, lanes(s)]
                exts[:, s, sbase:SC_PAD, :] = sc0_ref[:, :, lanes(s)]
        else:
            ret_ref[...] = jnp.zeros(ret_ref.shape, F32)
            extc[:, :, 0:CONF_PAD, :] = jnp.zeros((bs, n_slab, CONF_PAD, LANES), F32)
            exts[:, :, 0:SC_PAD, :] = jnp.zeros((bs, n_slab, SC_PAD, LANES), F32)

    x = x_ref[...]
    h = _rms(x, gmix_ref[...]) * (1.0 + scl_ref[...]) + sh_ref[...]
    hb = h.reshape(m, D_MODEL).astype(BF16)

    def proj(off, width):
        return _dot(hb, win_ref[:, off:off + width])

    q_s[...] = _rotate(proj(_O_Q, QK), cosq_ref[...], sinq_ref[...])
    k_rot = _rotate(proj(_O_K, QK), cosk_ref[...], sink_ref[...])
    v_s[...] = proj(_O_V, RV)
    g_s[...] = jax.nn.silu(proj(_O_G, RV))
    u = proj(_O_CA, CONF_W) * jax.nn.sigmoid(proj(_O_CB, CONF_W))
    us = proj(_O_SC, SC_W) * proj(_O_SX, SC_W)
    sb_s[...] = proj(_O_SB, SC_W)
    for s in range(n_slab):
        extc[:, s, CONF_PAD:CONF_PAD + tt, :] = u[:, lanes(s)].reshape(bs, tt, LANES)
        exts[:, s, SC_PAD:SC_PAD + tt, :] = us[:, lanes(s)].reshape(bs, tt, LANES)

    cv_slab = [None] * n_slab
    sv_slab = [None] * n_slab
    gate_cols = [None] * n_slab
    gw = 3 * D_MODEL // n_slab

    def zero_of(v):
        w = pltpu.bitcast(v, jnp.uint32)
        return pltpu.bitcast((w >> 16) >> 16, F32)

    def pair_task(s):
        def run():
            gate_cols[s] = jax.nn.sigmoid(proj(_O_GL + s * gw, gw) + bgate_ref[:, s * gw:(s + 1) * gw])
            acc = None
            for j in range(CONF_K):
                term = extc[:, s, cbase + j:cbase + j + tt, :] * ccw_ref[j:j + 1, lanes(s)]
                acc = term if acc is None else acc + term
            cv_slab[s] = acc.reshape(m, LANES) + zero_of(gate_cols[s][:, 0:LANES])
            acc = None
            for j in range(SC_K):
                term = exts[:, s, sbase + j:sbase + j + tt, :] * scw_ref[j:j + 1, lanes(s)]
                acc = term if acc is None else acc + term
            sv_slab[s] = acc.reshape(m, LANES)
        return run

    fillers = [pair_task(s) for s in range(n_slab)]
    n_fill = len(fillers)

    lane = lax.broadcasted_iota(jnp.int32, (CHUNK, LANES), 1)
    low_half = lane < DK
    row_seq = lax.broadcasted_iota(jnp.int32, (CHUNK, LANES), 0) // seg
    col_seq = lax.broadcasted_iota(jnp.int32, (DK, CHUNK), 1) // seg
    n_iter = (m // CHUNK) * N_HEADS
    it = 0
    for c in range(m // CHUNK):
        r0 = c * CHUNK
        s0 = c * CHUNK // tt
        kt_s[...] = k_rot[r0:r0 + CHUNK, :].T
        for hd in range(N_HEADS):
            p = hd // 2
            q_pair = q_s[r0:r0 + CHUNK, p * LANES:(p + 1) * LANES]
            qm = jnp.where(low_half if hd % 2 == 0 else ~low_half, q_pair, 0.0)
            qmb = qm.astype(BF16)
            kt_pair = kt_s[p * LANES:(p + 1) * LANES, :].astype(BF16)
            kt_h = kt_s[hd * DK:(hd + 1) * DK, :]
            v_h = v_s[r0:r0 + CHUNK, hd * DV:(hd + 1) * DV]
            scores = _dot(qmb, kt_pair) * dec_ref[hd]
            o = _dot(scores.astype(BF16), v_h.astype(BF16))
            vd = (v_h * kdec_ref[hd]).astype(BF16)
            if nseg == 1:
                s_pair = ret_ref[s0, 2 * p:2 * p + 2].reshape(2 * DK, DV)
                o = o + _dot(qmb, s_pair.astype(BF16)) * qdec_ref[hd]
                ret_ref[s0, hd] = ret_ref[s0, hd] * cd_ref[hd] + _dot(kt_h.astype(BF16), vd)
            else:
                q_exp = jnp.concatenate(
                    [jnp.where(row_seq == b, qm, 0.0) for b in range(nseg)], axis=1).astype(BF16)
                s_all = ret_ref[s0:s0 + nseg, 2 * p:2 * p + 2].reshape(nseg * 2 * DK, DV)
                o = o + _dot(q_exp, s_all.astype(BF16)) * qdec_ref[hd]
                kt_exp = jnp.concatenate(
                    [jnp.where(col_seq == b, kt_h, 0.0) for b in range(nseg)], axis=0).astype(BF16)
                upd = _dot(kt_exp, vd).reshape(nseg, DK, DV)
                ret_ref[s0:s0 + nseg, hd] = ret_ref[s0:s0 + nseg, hd] * cd_ref[hd] + upd
            mu = jnp.mean(o, axis=-1, keepdims=True)
            dev = o - mu
            var = jnp.mean(dev * dev, axis=-1, keepdims=True)
            y = dev * lax.rsqrt(var + 1e-5) * gng_ref[:, hd * DV:(hd + 1) * DV]
            gate = g_s[r0:r0 + CHUNK, hd * DV:(hd + 1) * DV]
            ro_s[r0:r0 + CHUNK, hd * DV:(hd + 1) * DV] = (gate * y).astype(BF16)
            it += 1
            while fillers and (n_fill - len(fillers)) * n_iter < it * n_fill:
                fillers.pop(0)()
    while fillers:
        fillers.pop(0)()

    gate_all = jnp.concatenate(gate_cols, axis=1)
    gates = [gate_all[:, i * D_MODEL:(i + 1) * D_MODEL] for i in range(3)]
    merged = gates[0] * _dot(ro_s[...], wret_ref[...])

    cv = jnp.concatenate(cv_slab, axis=1) + ccb_ref[...]
    mu = jnp.mean(cv, axis=-1, keepdims=True)
    dev = cv - mu
    var = jnp.mean(dev * dev, axis=-1, keepdims=True)
    cvn = dev * lax.rsqrt(var + 1e-5) * clg_ref[...] + clb_ref[...]
    merged = merged + gates[1] * _dot(jax.nn.silu(cvn).astype(BF16), wconf_ref[...])

    sv = jnp.concatenate(sv_slab, axis=1)
    merged = merged + gates[2] * _dot((sb_s[...] * sv).astype(BF16), wsc_ref[...])

    for s in range(n_slab):
        confo_ref[:, :, lanes(s)] = extc[:, s, tt + cbase:tt + CONF_PAD, :]
        sco_ref[:, :, lanes(s)] = exts[:, s, tt + sbase:tt + SC_PAD, :]
        if tt >= CONF_PAD:
            extc[:, s, 0:CONF_PAD, :] = extc[:, s, tt:tt + CONF_PAD, :]
            exts[:, s, 0:SC_PAD, :] = exts[:, s, tt:tt + SC_PAD, :]

    mo = _dot(merged.astype(BF16), wo_ref[...])
    xo_ref[...] = x + gt_ref[...] * mo.reshape(bs, tt, D_MODEL)


def _mixer(x, ada3, p, layer, bs, tt, tables, state=None, prev=None):
    b, t, d = x.shape
    has_state = state is not None
    m = bs * tt
    cosq, sinq, cosk, sink, dec, qdec, kdec, cd = tables

    def mod(k):
        return pl.BlockSpec((bs, 1, d), lambda i, j: (i, 0, k))

    if cosq.shape[0] == m:
        rope = pl.BlockSpec((m, LANES), lambda i, j: (0, 0))
    else:
        rope = pl.BlockSpec((tt, LANES), lambda i, j: (j, 0))

    state_specs = [
        pl.BlockSpec((None, bs, N_HEADS, DK, DV), lambda i, j: (layer, i, 0, 0, 0)),
        pl.BlockSpec((None, bs, CONF_K - 1, CONF_W), lambda i, j: (layer, i, 0, 0)),
        pl.BlockSpec((None, bs, SC_K - 1, SC_W), lambda i, j: (layer, i, 0, 0)),
    ]
    weights = [p["g_mix"], p["w_in"], p["b_gate"], p["ret_gn_g"], p["w_ret_out"],
               p["conf_conv_w"], p["conf_conv_b"], p["conf_ln_g"], p["conf_ln_b"],
               p["w_conf_out"], p["sc_conv_w"], p["w_sc_out"], p["w_o"]]
    in_specs = [pl.BlockSpec((bs, tt, d), lambda i, j: (i, j, 0)), mod(3), mod(4), mod(5)]
    in_specs += [_resident(w.shape, layer) for w in weights]
    in_specs += [rope] * 4 + [_resident(a.shape) for a in (dec, qdec, kdec, cd)]
    args = [x, ada3, ada3, ada3] + weights + [cosq, sinq, cosk, sink, dec, qdec, kdec, cd]
    if has_state:
        in_specs += state_specs
        args += list(state)
    aliases = {}
    if prev is not None:
        for k, a in enumerate(prev):
            aliases[len(args)] = 1 + k
            in_specs.append(pl.BlockSpec(memory_space=pl.ANY))
            args.append(a)
    out_specs = [pl.BlockSpec((bs, tt, d), lambda i, j: (i, j, 0))] + state_specs
    out_shape = [
        jax.ShapeDtypeStruct(x.shape, F32),
        jax.ShapeDtypeStruct((DEPTH, b, N_HEADS, DK, DV), F32),
        jax.ShapeDtypeStruct((DEPTH, b, CONF_K - 1, CONF_W), F32),
        jax.ShapeDtypeStruct((DEPTH, b, SC_K - 1, SC_W), F32),
    ]
    scratch = [
        pltpu.VMEM((m, QK), F32),
        pltpu.VMEM((QK, CHUNK), F32),
        pltpu.VMEM((m, RV), F32),
        pltpu.VMEM((m, RV), F32),
        pltpu.VMEM((m, RV), BF16),
        pltpu.VMEM((m, SC_W), F32),
        pltpu.VMEM((bs, CONF_W // LANES, CONF_PAD + tt, LANES), F32),
        pltpu.VMEM((bs, SC_W // LANES, SC_PAD + tt, LANES), F32),
    ]
    return pl.pallas_call(
        functools.partial(_mixer_kernel, bs=bs, tt=tt, has_state=has_state,
                          n_alias=len(aliases)),
        grid=(b // bs, t // tt),
        in_specs=in_specs,
        out_specs=out_specs,
        out_shape=out_shape,
        scratch_shapes=scratch,
        input_output_aliases=aliases,
        compiler_params=pltpu.CompilerParams(
            dimension_semantics=("arbitrary", "arbitrary"), vmem_limit_bytes=VMEM_LIMIT),
        name="mixer_state" if has_state else "mixer",
    )(*args)


def _rope_tables(pos):
    half = DK // 2
    freqs = ROPE_BASE ** (-jnp.arange(half, dtype=F32) / half)
    ang = pos[:, None] * freqs[None, :]
    cos, sin = jnp.cos(ang), jnp.sin(ang)
    reps = LANES // DK
    cos_l = jnp.tile(jnp.concatenate([cos, cos], axis=1), (1, reps))
    sin_l = jnp.tile(jnp.concatenate([-sin, sin], axis=1), (1, reps))
    scale = DK ** -0.5
    return cos_l * scale, sin_l * scale, cos_l, sin_l


def _decay_tables(chunk):
    lg = jnp.log(1.0 - jnp.exp2(-5.0 - jnp.arange(N_HEADS, dtype=F32)))
    row = jnp.arange(CHUNK)
    idx = (row % chunk).astype(F32)
    seq = row // chunk
    diff = idx[:, None] - idx[None, :]
    keep = (seq[:, None] == seq[None, :]) & (diff >= 0)
    dec = jnp.where(keep[None], jnp.exp(jnp.maximum(diff, 0.0)[None] * lg[:, None, None]), 0.0)
    q_decay = jnp.exp((idx + 1.0)[None, :] * lg[:, None])
    k_decay = jnp.exp((chunk - 1.0 - idx)[None, :] * lg[:, None])
    chunk_decay = jnp.exp(chunk * lg)
    qdec = jnp.broadcast_to(q_decay[:, :, None], (N_HEADS, CHUNK, DV))
    kdec = jnp.broadcast_to(k_decay[:, :, None], (N_HEADS, CHUNK, DV))
    cd = jnp.broadcast_to(chunk_decay[:, None, None], (N_HEADS, 1, DV))
    return dec, qdec, kdec, cd


def kernel(x_prompt, x_sample, c_prompt, c_sample, state_ret, state_conf, state_sconv,
           w_ada, b_ada, g_ffn1, w1_a, w3_a, w2_a, g_mix, w_in, b_gate, ret_gn_g, w_ret_out,
           conf_conv_w, conf_conv_b, conf_ln_g, conf_ln_b, w_conf_out, sc_conv_w, w_sc_out, w_o,
           g_ffn2, w1_b, w3_b, w2_b, g_final):
    nbp, seq, _ = x_prompt.shape
    nbs, dec_seq, _ = x_sample.shape

    def row(a):
        return a.reshape(DEPTH, 1, a.shape[-1])

    p = dict(
        g_mix=row(g_mix), w_in=w_in.astype(BF16), b_gate=row(b_gate), ret_gn_g=row(ret_gn_g),
        w_ret_out=w_ret_out.astype(BF16), conf_conv_w=conf_conv_w, conf_conv_b=row(conf_conv_b),
        conf_ln_g=row(conf_ln_g), conf_ln_b=row(conf_ln_b), w_conf_out=w_conf_out.astype(BF16),
        sc_conv_w=sc_conv_w, w_sc_out=w_sc_out.astype(BF16), w_o=w_o.astype(BF16))
    ffn_a = (row(g_ffn1), w1_a.astype(BF16), w3_a.astype(BF16), w2_a.astype(BF16))
    ffn_b = (row(g_ffn2), w1_b.astype(BF16), w3_b.astype(BF16), w2_b.astype(BF16))
    g_fin = g_final.reshape(1, D_MODEL)

    ada = _ada(jnp.concatenate([c_prompt, c_sample], axis=0), w_ada, b_ada)

    chunk_p = RET_CHUNK if seq % RET_CHUNK == 0 else seq
    tab_p = _rope_tables(jnp.arange(seq, dtype=F32)) + _decay_tables(chunk_p)
    chunk_s = RET_CHUNK if dec_seq % RET_CHUNK == 0 else dec_seq
    bs_mix = CHUNK // dec_seq
    rope_s = _rope_tables(PAST_LEN + jnp.arange(dec_seq, dtype=F32))
    tab_s = tuple(jnp.tile(a, (bs_mix, 1)) for a in rope_s) + _decay_tables(chunk_s)

    def trunk(x, ada_g, ffn_bs, ffn_tt, mix_bs, mix_tt, tables, states):
        new_states = None
        for l in range(DEPTH):
            ada3 = ada_g[l][:, None, :]
            x = _ffn(x, ada3, 0, *ffn_a, l, ffn_bs, ffn_tt)
            x, *new_states = _mixer(x, ada3, p, l, mix_bs, mix_tt, tables, states, new_states)
            x = _ffn(x, ada3, 6, *ffn_b, l, ffn_bs, ffn_tt,
                     g_final=g_fin if l == DEPTH - 1 else None)
        return (x, *new_states)

    y_p, ret_p, conf_p, sc_p = trunk(x_prompt, ada[:, :nbp], 1, 1024, 1, 512, tab_p, None)
    y_s, ret_s, conf_s, sc_s = trunk(x_sample, ada[:, nbp:], 64, dec_seq, bs_mix, dec_seq, tab_s,
                                     (state_ret, state_conf, state_sconv))
    return (y_p, y_s, ret_p, ret_s, conf_p, conf_s, sc_p, sc_s)
```

```python
import functools

import jax
import jax.numpy as jnp
from jax import lax
from jax.experimental import pallas as pl
from jax.experimental.pallas import tpu as pltpu

D_MODEL = 1024
DEPTH = 2
PAST_LEN = 16384
N_HEADS = 8
DK = 64
DV = 128
QK = N_HEADS * DK
RV = N_HEADS * DV
RET_CHUNK = 128
ROPE_BASE = 10000.0
CONF_W = 512
CONF_K = 31
SC_W = 512
SC_K = 3
D_FF = 2816
N_ADA = 9

_O_Q, _O_K, _O_V, _O_G = 0, QK, 2 * QK, 2 * QK + RV
_O_CA = 2 * QK + 2 * RV
_O_CB = _O_CA + CONF_W
_O_SB = _O_CB + CONF_W
_O_SC = _O_SB + SC_W
_O_SX = _O_SC + SC_W
_O_GL = _O_SX + SC_W
IN_COLS = _O_GL + 3 * D_MODEL

LANES = 128
BF16_ROWS = 16
CHUNK = 128
CONF_PAD = 32
SC_PAD = 8
MXU_TILE = 256
FF_SPLITS = ((0, 6 * MXU_TILE), (6 * MXU_TILE, D_FF))
VMEM_LIMIT = 56 * 1024 * 1024

BF16 = jnp.bfloat16
F32 = jnp.float32


def _dot(a, b):
    return jnp.dot(a, b, preferred_element_type=F32)


def _resident(shape, layer=None):
    if layer is None:
        nd = len(shape)
        return pl.BlockSpec(tuple(shape), lambda *_: (0,) * nd, pipeline_mode=pl.Buffered(1))
    nd = len(shape) - 1
    return pl.BlockSpec((None,) + tuple(shape[1:]), lambda *_: (layer,) + (0,) * nd,
                        pipeline_mode=pl.Buffered(1))


def _cast_plan(ws, layer, grid):
    n_steps = grid[0] * grid[1]
    in_specs, out_specs, out_shapes = [], [], []
    for w in ws:
        _, r, c = w.shape
        nblk = max(n for n in range(1, n_steps + 1)
                   if n_steps % n == 0 and r % n == 0 and (r // n) % BF16_ROWS == 0)
        rep, rb = n_steps // nblk, r // nblk
        in_specs.append(pl.BlockSpec(
            (None, rb, c), lambda i, j, rep=rep: (layer, (i * grid[1] + j) // rep, 0)))
        out_specs.append(pl.BlockSpec(
            (rb, c), lambda i, j, rep=rep: ((i * grid[1] + j) // rep, 0)))
        out_shapes.append(jax.ShapeDtypeStruct((r, c), BF16))
    return in_specs, out_specs, out_shapes


def _rms(x, g):
    ms = jnp.mean(x * x, axis=-1, keepdims=True)
    return x * lax.rsqrt(ms + 1e-6) * g


def _ada_kernel(c_ref, w_ref, b_ref, o_ref):
    s = jax.nn.silu(c_ref[...]).astype(BF16)
    o_ref[...] = _dot(s, w_ref[...].astype(BF16)) + b_ref[...]


def _ada(c_all, w_ada, b_ada):
    nb = c_all.shape[0]
    tn = D_MODEL
    return pl.pallas_call(
        _ada_kernel,
        grid=(DEPTH, N_ADA * D_MODEL // tn),
        in_specs=[
            pl.BlockSpec((nb, D_MODEL), lambda l, j: (0, 0)),
            pl.BlockSpec((None, D_MODEL, tn), lambda l, j: (l, 0, j)),
            pl.BlockSpec((None, 1, tn), lambda l, j: (l, 0, j)),
        ],
        out_specs=pl.BlockSpec((None, nb, tn), lambda l, j: (l, 0, j)),
        out_shape=jax.ShapeDtypeStruct((DEPTH, nb, N_ADA * D_MODEL), F32),
        compiler_params=pltpu.CompilerParams(dimension_semantics=("arbitrary", "arbitrary")),
        name="ada",
    )(c_all, w_ada, b_ada.reshape(DEPTH, 1, N_ADA * D_MODEL))


def _ffn_kernel(*refs, final, n_cast):
    x_ref, sh_ref, sc_ref, gt_ref, g_ref, w1_ref, w3_ref, w2_ref = refs[:8]
    n_in = 9 if final else 8
    gf_ref = refs[8] if final else None
    cast_in = refs[n_in:n_in + n_cast]
    o_ref = refs[n_in + n_cast]
    cast_out = refs[n_in + n_cast + 1:]
    bs, tt, d = x_ref.shape
    x = x_ref[...]
    h = _rms(x, g_ref[...]) * (1.0 + sc_ref[...]) + sh_ref[...]
    hb = h.reshape(bs * tt, d).astype(BF16)
    y = None
    for lo, hi in FF_SPLITS:
        a = _dot(hb, w1_ref[:, lo:hi])
        b = _dot(hb, w3_ref[:, lo:hi])
        act = (jax.nn.silu(a) * b).astype(BF16)
        yj = _dot(act, w2_ref[lo:hi, :])
        y = yj if y is None else y + yj
    out = x + (0.5 * gt_ref[...]) * y.reshape(bs, tt, d)
    if final:
        out = _rms(out, gf_ref[...])
    o_ref[...] = out
    for src, dst in zip(cast_in, cast_out):
        dst[...] = src[...].astype(BF16)


def _ffn(x, ada3, k0, g, w, layer, bs, tt, g_final=None, cast=(), cast_layer=None):
    b, t, d = x.shape
    final = g_final is not None
    grid = (b // bs, t // tt)

    def mod(k):
        return pl.BlockSpec((bs, 1, d), lambda i, j: (i, 0, k))

    in_specs = [pl.BlockSpec((bs, tt, d), lambda i, j: (i, j, 0)), mod(k0), mod(k0 + 1), mod(k0 + 2),
                _resident(g.shape, layer)] + [_resident(a.shape) for a in w]
    args = [x, ada3, ada3, ada3, g, *w]
    if final:
        in_specs.append(_resident(g_final.shape))
        args.append(g_final)
    c_in, c_out, c_shape = _cast_plan(cast, layer if cast_layer is None else cast_layer, grid)
    outs = pl.pallas_call(
        functools.partial(_ffn_kernel, final=final, n_cast=len(cast)),
        grid=grid,
        in_specs=in_specs + c_in,
        out_specs=[pl.BlockSpec((bs, tt, d), lambda i, j: (i, j, 0))] + c_out,
        out_shape=[jax.ShapeDtypeStruct(x.shape, F32)] + c_shape,
        compiler_params=pltpu.CompilerParams(
            dimension_semantics=("arbitrary", "arbitrary"), vmem_limit_bytes=VMEM_LIMIT),
        name="ffn_final" if final else "ffn",
    )(*args, *cast)
    return outs[0], tuple(outs[1:])


def _rotate(z, cos, sin_signed):
    lane = lax.broadcasted_iota(jnp.int32, z.shape, 1)
    first = (lane % DK) < (DK // 2)
    partner = jnp.where(first, pltpu.roll(z, QK - DK // 2, 1), pltpu.roll(z, DK // 2, 1))
    reps = QK // LANES
    cos4 = jnp.concatenate([cos] * reps, axis=1)
    sin4 = jnp.concatenate([sin_signed] * reps, axis=1)
    return z * cos4 + partner * sin4


def _mixer_kernel(*refs, bs, tt, has_state, n_alias, n_cast):
    (x_ref, sh_ref, scl_ref, gt_ref, gmix_ref, win_ref, bgate_ref, gng_ref, wret_ref,
     ccw_ref, ccb_ref, clg_ref, clb_ref, wconf_ref, scw_ref, wsc_ref, wo_ref,
     cosq_ref, sinq_ref, cosk_ref, sink_ref, dec_ref, qdec_ref, kdec_ref, cd_ref) = refs[:25]
    n_in = 25
    if has_state:
        ret0_ref, conf0_ref, sc0_ref = refs[25:28]
        n_in = 28
    n_in += n_alias
    cast_in = refs[n_in:n_in + n_cast]
    n_in += n_cast
    xo_ref, ret_ref, confo_ref, sco_ref = refs[n_in:n_in + 4]
    cast_out = refs[n_in + 4:n_in + 4 + n_cast]
    q_s, kt_s, v_s, g_s, ro_s, sb_s, extc, exts = refs[n_in + 4 + n_cast:]
    m = bs * tt
    seg = tt if tt < CHUNK else CHUNK
    nseg = CHUNK // seg
    n_slab = CONF_W // LANES
    cbase = CONF_PAD - (CONF_K - 1)
    sbase = SC_PAD - (SC_K - 1)
    t_idx = pl.program_id(1)

    def lanes(s):
        return slice(s * LANES, (s + 1) * LANES)

    @pl.when(t_idx == 0)
    def _():
        if has_state:
            ret_ref[...] = ret0_ref[...]
            for s in range(n_slab):
                extc[:, s, cbase:CONF_PAD, :] = conf0_ref[:, :, lanes(s)]
                exts[:, s, sbase:SC_PAD, :] = sc0_ref[:, :, lanes(s)]
        else:
            ret_ref[...] = jnp.zeros(ret_ref.shape, F32)
            extc[:, :, 0:CONF_PAD, :] = jnp.zeros((bs, n_slab, CONF_PAD, LANES), F32)
            exts[:, :, 0:SC_PAD, :] = jnp.zeros((bs, n_slab, SC_PAD, LANES), F32)

    x = x_ref[...]
    h = _rms(x, gmix_ref[...]) * (1.0 + scl_ref[...]) + sh_ref[...]
    hb = h.reshape(m, D_MODEL).astype(BF16)

    def proj(off, width):
        return _dot(hb, win_ref[:, off:off + width])

    q_s[...] = _rotate(proj(_O_Q, QK), cosq_ref[...], sinq_ref[...])
    k_rot = _rotate(proj(_O_K, QK), cosk_ref[...], sink_ref[...])
    v_s[...] = proj(_O_V, RV)
    g_s[...] = jax.nn.silu(proj(_O_G, RV))
    u = proj(_O_CA, CONF_W) * jax.nn.sigmoid(proj(_O_CB, CONF_W))
    us = proj(_O_SC, SC_W) * proj(_O_SX, SC_W)
    sb_s[...] = proj(_O_SB, SC_W)
    for s in range(n_slab):
        extc[:, s, CONF_PAD:CONF_PAD + tt, :] = u[:, lanes(s)].reshape(bs, tt, LANES)
        exts[:, s, SC_PAD:SC_PAD + tt, :] = us[:, lanes(s)].reshape(bs, tt, LANES)

    cv_slab = [None] * n_slab
    sv_slab = [None] * n_slab
    gate_cols = [None] * n_slab
    gw = 3 * D_MODEL // n_slab

    def zero_of(v):
        w = pltpu.bitcast(v, jnp.uint32)
        return pltpu.bitcast((w >> 16) >> 16, F32)

    def pair_task(s):
        def run():
            gate_cols[s] = jax.nn.sigmoid(proj(_O_GL + s * gw, gw) + bgate_ref[:, s * gw:(s + 1) * gw])
            acc = None
            for j in range(CONF_K):
                term = extc[:, s, cbase + j:cbase + j + tt, :] * ccw_ref[j:j + 1, lanes(s)]
                acc = term if acc is None else acc + term
            cv_slab[s] = acc.reshape(m, LANES) + zero_of(gate_cols[s][:, 0:LANES])
            acc = None
            for j in range(SC_K):
                term = exts[:, s, sbase + j:sbase + j + tt, :] * scw_ref[j:j + 1, lanes(s)]
                acc = term if acc is None else acc + term
            sv_slab[s] = acc.reshape(m, LANES)
        return run

    fillers = [pair_task(s) for s in range(n_slab)]
    n_fill = len(fillers)

    lane = lax.broadcasted_iota(jnp.int32, (CHUNK, LANES), 1)
    low_half = lane < DK
    row_seq = lax.broadcasted_iota(jnp.int32, (CHUNK, LANES), 0) // seg
    col_seq = lax.broadcasted_iota(jnp.int32, (DK, CHUNK), 1) // seg
    n_iter = (m // CHUNK) * N_HEADS
    it = 0
    for c in range(m // CHUNK):
        r0 = c * CHUNK
        s0 = c * CHUNK // tt
        kt_s[...] = k_rot[r0:r0 + CHUNK, :].T
        for hd in range(N_HEADS):
            p = hd // 2
            q_pair = q_s[r0:r0 + CHUNK, p * LANES:(p + 1) * LANES]
            qm = jnp.where(low_half if hd % 2 == 0 else ~low_half, q_pair, 0.0)
            qmb = qm.astype(BF16)
            kt_pair = kt_s[p * LANES:(p + 1) * LANES, :].astype(BF16)
            kt_h = kt_s[hd * DK:(hd + 1) * DK, :]
            v_h = v_s[r0:r0 + CHUNK, hd * DV:(hd + 1) * DV]
            scores = _dot(qmb, kt_pair) * dec_ref[hd]
            o = _dot(scores.astype(BF16), v_h.astype(BF16))
            vd = (v_h * kdec_ref[hd]).astype(BF16)
            if nseg == 1:
                s_pair = ret_ref[s0, 2 * p:2 * p + 2].reshape(2 * DK, DV)
                o = o + _dot(qmb, s_pair.astype(BF16)) * qdec_ref[hd]
                ret_ref[s0, hd] = ret_ref[s0, hd] * cd_ref[hd] + _dot(kt_h.astype(BF16), vd)
            else:
                q_exp = jnp.concatenate(
                    [jnp.where(row_seq == b, qm, 0.0) for b in range(nseg)], axis=1).astype(BF16)
                s_all = ret_ref[s0:s0 + nseg, 2 * p:2 * p + 2].reshape(nseg * 2 * DK, DV)
                o = o + _dot(q_exp, s_all.astype(BF16)) * qdec_ref[hd]
                kt_exp = jnp.concatenate(
                    [jnp.where(col_seq == b, kt_h, 0.0) for b in range(nseg)], axis=0).astype(BF16)
                upd = _dot(kt_exp, vd).reshape(nseg, DK, DV)
                ret_ref[s0:s0 + nseg, hd] = ret_ref[s0:s0 + nseg, hd] * cd_ref[hd] + upd
            mu = jnp.mean(o, axis=-1, keepdims=True)
            dev = o - mu
            var = jnp.mean(dev * dev, axis=-1, keepdims=True)
            y = dev * lax.rsqrt(var + 1e-5) * gng_ref[:, hd * DV:(hd + 1) * DV]
            gate = g_s[r0:r0 + CHUNK, hd * DV:(hd + 1) * DV]
            ro_s[r0:r0 + CHUNK, hd * DV:(hd + 1) * DV] = (gate * y).astype(BF16)
            it += 1
            while fillers and (n_fill - len(fillers)) * n_iter < it * n_fill:
                fillers.pop(0)()
    while fillers:
        fillers.pop(0)()

    gate_all = jnp.concatenate(gate_cols, axis=1)
    gates = [gate_all[:, i * D_MODEL:(i + 1) * D_MODEL] for i in range(3)]
    merged = gates[0] * _dot(ro_s[...], wret_ref[...])

    cv = jnp.concatenate(cv_slab, axis=1) + ccb_ref[...]
    mu = jnp.mean(cv, axis=-1, keepdims=True)
    dev = cv - mu
    var = jnp.mean(dev * dev, axis=-1, keepdims=True)
    cvn = dev * lax.rsqrt(var + 1e-5) * clg_ref[...] + clb_ref[...]
    merged = merged + gates[1] * _dot(jax.nn.silu(cvn).astype(BF16), wconf_ref[...])

    sv = jnp.concatenate(sv_slab, axis=1)
    merged = merged + gates[2] * _dot((sb_s[...] * sv).astype(BF16), wsc_ref[...])

    for s in range(n_slab):
        confo_ref[:, :, lanes(s)] = extc[:, s, tt + cbase:tt + CONF_PAD, :]
        sco_ref[:, :, lanes(s)] = exts[:, s, tt + sbase:tt + SC_PAD, :]
        if tt >= CONF_PAD:
            extc[:, s, 0:CONF_PAD, :] = extc[:, s, tt:tt + CONF_PAD, :]
            exts[:, s, 0:SC_PAD, :] = exts[:, s, tt:tt + SC_PAD, :]

    mo = _dot(merged.astype(BF16), wo_ref[...])
    xo_ref[...] = x + gt_ref[...] * mo.reshape(bs, tt, D_MODEL)
    for src, dst in zip(cast_in, cast_out):
        dst[...] = src[...].astype(BF16)


def _mixer(x, ada3, p, wb, layer, bs, tt, tables, prev, state=None, cast=()):
    b, t, d = x.shape
    has_state = state is not None
    m = bs * tt
    grid = (b // bs, t // tt)
    cosq, sinq, cosk, sink, dec, qdec, kdec, cd = tables
    w_in, w_ret, w_conf, w_sc, w_o = wb

    def mod(k):
        return pl.BlockSpec((bs, 1, d), lambda i, j: (i, 0, k))

    if cosq.shape[0] == m:
        rope = pl.BlockSpec((m, LANES), lambda i, j: (0, 0))
    else:
        rope = pl.BlockSpec((tt, LANES), lambda i, j: (j, 0))

    state_specs = [
        pl.BlockSpec((None, bs, N_HEADS, DK, DV), lambda i, j: (layer, i, 0, 0, 0)),
        pl.BlockSpec((None, bs, CONF_K - 1, CONF_W), lambda i, j: (layer, i, 0, 0)),
        pl.BlockSpec((None, bs, SC_K - 1, SC_W), lambda i, j: (layer, i, 0, 0)),
    ]

    def small(name):
        return p[name], _resident(p[name].shape, layer)

    def big(w):
        return w, _resident(w.shape)

    weights = [small("g_mix"), big(w_in), small("b_gate"), small("ret_gn_g"), big(w_ret),
               small("conf_conv_w"), small("conf_conv_b"), small("conf_ln_g"), small("conf_ln_b"),
               big(w_conf), small("sc_conv_w"), big(w_sc), big(w_o)]
    in_specs = [pl.BlockSpec((bs, tt, d), lambda i, j: (i, j, 0)), mod(3), mod(4), mod(5)]
    in_specs += [spec for _, spec in weights]
    in_specs += [rope] * 4 + [_resident(a.shape) for a in (dec, qdec, kdec, cd)]
    args = [x, ada3, ada3, ada3] + [a for a, _ in weights] + [cosq, sinq, cosk, sink, dec, qdec, kdec, cd]
    if has_state:
        in_specs += state_specs
        args += list(state)
    aliases = {}
    for k, a in enumerate(prev):
        aliases[len(args)] = 1 + k
        in_specs.append(pl.BlockSpec(memory_space=pl.ANY))
        args.append(a)
    c_in, c_out, c_shape = _cast_plan(cast, layer, grid)
    out_specs = [pl.BlockSpec((bs, tt, d), lambda i, j: (i, j, 0))] + state_specs + c_out
    out_shape = [jax.ShapeDtypeStruct(x.shape, F32)] + [
        jax.ShapeDtypeStruct(a.shape, F32) for a in prev] + c_shape
    scratch = [
        pltpu.VMEM((m, QK), F32),
        pltpu.VMEM((QK, CHUNK), F32),
        pltpu.VMEM((m, RV), F32),
        pltpu.VMEM((m, RV), F32),
        pltpu.VMEM((m, RV), BF16),
        pltpu.VMEM((m, SC_W), F32),
        pltpu.VMEM((bs, CONF_W // LANES, CONF_PAD + tt, LANES), F32),
        pltpu.VMEM((bs, SC_W // LANES, SC_PAD + tt, LANES), F32),
    ]
    outs = pl.pallas_call(
        functools.partial(_mixer_kernel, bs=bs, tt=tt, has_state=has_state,
                          n_alias=len(aliases), n_cast=len(cast)),
        grid=grid,
        in_specs=in_specs + c_in,
        out_specs=out_specs,
        out_shape=out_shape,
        scratch_shapes=scratch,
        input_output_aliases=aliases,
        compiler_params=pltpu.CompilerParams(
            dimension_semantics=("arbitrary", "arbitrary"), vmem_limit_bytes=VMEM_LIMIT),
        name="mixer_state" if has_state else "mixer",
    )(*args, *cast)
    return outs[0], tuple(outs[1:4]), tuple(outs[4:])


def _rope_tables(pos):
    half = DK // 2
    freqs = ROPE_BASE ** (-jnp.arange(half, dtype=F32) / half)
    ang = pos[:, None] * freqs[None, :]
    cos, sin = jnp.cos(ang), jnp.sin(ang)
    reps = LANES // DK
    cos_l = jnp.tile(jnp.concatenate([cos, cos], axis=1), (1, reps))
    sin_l = jnp.tile(jnp.concatenate([-sin, sin], axis=1), (1, reps))
    scale = DK ** -0.5
    return cos_l * scale, sin_l * scale, cos_l, sin_l


def _decay_tables(chunk):
    lg = jnp.log(1.0 - jnp.exp2(-5.0 - jnp.arange(N_HEADS, dtype=F32)))
    row = jnp.arange(CHUNK)
    idx = (row % chunk).astype(F32)
    seq = row // chunk
    diff = idx[:, None] - idx[None, :]
    keep = (seq[:, None] == seq[None, :]) & (diff >= 0)
    dec = jnp.where(keep[None], jnp.exp(jnp.maximum(diff, 0.0)[None] * lg[:, None, None]), 0.0)
    q_decay = jnp.exp((idx + 1.0)[None, :] * lg[:, None])
    k_decay = jnp.exp((chunk - 1.0 - idx)[None, :] * lg[:, None])
    chunk_decay = jnp.exp(chunk * lg)
    qdec = jnp.broadcast_to(q_decay[:, :, None], (N_HEADS, CHUNK, DV))
    kdec = jnp.broadcast_to(k_decay[:, :, None], (N_HEADS, CHUNK, DV))
    cd = jnp.broadcast_to(chunk_decay[:, None, None], (N_HEADS, 1, DV))
    return dec, qdec, kdec, cd


def kernel(x_prompt, x_sample, c_prompt, c_sample, state_ret, state_conf, state_sconv,
           w_ada, b_ada, g_ffn1, w1_a, w3_a, w2_a, g_mix, w_in, b_gate, ret_gn_g, w_ret_out,
           conf_conv_w, conf_conv_b, conf_ln_g, conf_ln_b, w_conf_out, sc_conv_w, w_sc_out, w_o,
           g_ffn2, w1_b, w3_b, w2_b, g_final):
    nbp, seq, _ = x_prompt.shape
    nbs, dec_seq, _ = x_sample.shape

    def row(a):
        return a.reshape(DEPTH, 1, a.shape[-1])

    p = dict(g_mix=row(g_mix), b_gate=row(b_gate), ret_gn_g=row(ret_gn_g), conf_conv_w=conf_conv_w,
             conf_conv_b=row(conf_conv_b), conf_ln_g=row(conf_ln_g), conf_ln_b=row(conf_ln_b),
             sc_conv_w=sc_conv_w)
    g1, g2, g_fin = row(g_ffn1), row(g_ffn2), g_final.reshape(1, D_MODEL)
    ffn_a_f32 = (w1_a, w3_a, w2_a)
    ffn_b_f32 = (w1_b, w3_b, w2_b)
    mix_f32 = (w_in, w_ret_out, w_conf_out, w_sc_out, w_o)

    ada = _ada(jnp.concatenate([c_prompt, c_sample], axis=0), w_ada, b_ada)

    chunk_p = RET_CHUNK if seq % RET_CHUNK == 0 else seq
    tab_p = _rope_tables(jnp.arange(seq, dtype=F32)) + _decay_tables(chunk_p)
    chunk_s = RET_CHUNK if dec_seq % RET_CHUNK == 0 else dec_seq
    bs_mix = CHUNK // dec_seq
    rope_s = _rope_tables(PAST_LEN + jnp.arange(dec_seq, dtype=F32))
    tab_s = tuple(jnp.tile(a, (bs_mix, 1)) for a in rope_s) + _decay_tables(chunk_s)

    def empty_states(nb):
        return (jnp.zeros((DEPTH, nb, N_HEADS, DK, DV), F32),
                jnp.zeros((DEPTH, nb, CONF_K - 1, CONF_W), F32),
                jnp.zeros((DEPTH, nb, SC_K - 1, SC_W), F32))

    wa = [tuple(w[0].astype(BF16) for w in ffn_a_f32)] + [None] * (DEPTH - 1)
    wm, wb_ = [None] * DEPTH, [None] * DEPTH
    x = x_prompt
    st_p = empty_states(nbp)
    for l in range(DEPTH):
        ada3 = ada[l, :nbp][:, None, :]
        x, wm[l] = _ffn(x, ada3, 0, g1, wa[l], l, 1, 1024, cast=mix_f32)
        x, st_p, wb_[l] = _mixer(x, ada3, p, wm[l], l, 1, 512, tab_p, st_p, cast=ffn_b_f32)
        last = l == DEPTH - 1
        x, nxt = _ffn(x, ada3, 6, g2, wb_[l], l, 1, 1024, g_final=g_fin if last else None,
                      cast=() if last else ffn_a_f32, cast_layer=l + 1)
        if not last:
            wa[l + 1] = nxt
    y_p = x

    x = x_sample
    st_s = empty_states(nbs)
    states = (state_ret, state_conf, state_sconv)
    for l in range(DEPTH):
        ada3 = ada[l, nbp:][:, None, :]
        x, _ = _ffn(x, ada3, 0, g1, wa[l], l, 64, dec_seq)
        x, st_s, _ = _mixer(x, ada3, p, wm[l], l, bs_mix, dec_seq, tab_s, st_s, state=states)
        x, _ = _ffn(x, ada3, 6, g2, wb_[l], l, 64, dec_seq,
                    g_final=g_fin if l == DEPTH - 1 else None)
    y_s = x
    return (y_p, y_s, st_p[0], st_s[0], st_p[1], st_s[1], st_p[2], st_s[2])
```
